```python
import math
import jax, jax.numpy as jnp
from jax import lax
import numpy as np

D_MODEL = 1024
BATCH = 2
SEQ = 8192
DEPTH = 2
DEC_BATCH = 8
DEC_SEQ = 32
PAST_LEN = 2048

CHUNK = 64
POOL_WINDOWS = (2, 4, 8, 16)
N_POOL_GROUPS = 4
D_POOL = 512
POOL_GROUP = D_POOL // N_POOL_GROUPS
POOL_HIST = max(POOL_WINDOWS) - 1
N_RET_HEADS = 8
DK_RET = 128
DV_RET = 128
D_RET_QK = N_RET_HEADS * DK_RET
D_RET_V = N_RET_HEADS * DV_RET
RET_SCALE = DK_RET ** -0.5
ROPE_BASE = 10000.0
PEER_HEADS = 8
PEER_NKEYS = 128
PEER_N = PEER_NKEYS * PEER_NKEYS
PEER_DQ = 256
PEER_DHALF = PEER_DQ // 2
PEER_TOPK = 16
PEER_BLOCK = 512
LN_EPS = 1e-5
DEEPNORM_ALPHA = (2 * DEPTH) ** 0.25
DEEPNORM_BETA = (8 * DEPTH) ** -0.25
OFF_Q = D_POOL
OFF_K = OFF_Q + D_RET_QK
OFF_V = OFF_K + D_RET_QK
OFF_G = OFF_V + D_RET_V
OFF_BG = OFF_G + D_RET_V
D_IN = OFF_BG + 2 * D_MODEL

kernel_name = 'hybrid_pool_retention_peer_stream_step'


def layer_norm(x, gain=None, bias=None):
    xf = x.astype(jnp.float32)
    mu = jnp.mean(xf, axis=-1, keepdims=True)
    var = jnp.mean(jnp.square(xf - mu), axis=-1, keepdims=True)
    y = (xf - mu) * lax.rsqrt(var + LN_EPS)
    if gain is not None:
        y = y * gain.astype(jnp.float32) + bias.astype(jnp.float32)
    return y.astype(x.dtype)


def pool_mixer(p, hist, pos0, w_grp, scale):
    bsz, L, _ = p.shape
    full = jnp.concatenate([hist.astype(p.dtype), p], axis=1)
    cs = jnp.pad(jnp.cumsum(full.astype(jnp.float32), axis=1), ((0, 0), (1, 0), (0, 0)))
    pos = pos0 + jnp.arange(L)
    outs = []
    for g, w in enumerate(POOL_WINDOWS):
        sl = slice(g * POOL_GROUP, (g + 1) * POOL_GROUP)
        hi = cs[:, POOL_HIST + 1:POOL_HIST + 1 + L, sl]
        lo = cs[:, POOL_HIST + 1 - w:POOL_HIST + 1 - w + L, sl]
        cnt = jnp.minimum(pos + 1, w).astype(jnp.float32)[None, :, None]
        outs.append((hi - lo) / cnt - full[:, POOL_HIST:, sl].astype(jnp.float32))
    pooled = jnp.stack(outs, axis=2).astype(p.dtype)
    mixed = jnp.einsum('blgc,gcd->blgd', pooled, w_grp).reshape(bsz, L, D_POOL) * scale
    return mixed, full[:, -POOL_HIST:]


def rotary(x, pos):
    half = x.shape[-1] // 2
    inv = ROPE_BASE ** (-jnp.arange(half, dtype=jnp.float32) / half)
    ang = pos.astype(jnp.float32)[:, None] * inv[None, :]
    cos = jnp.cos(ang)[None, :, None, :]
    sin = jnp.sin(ang)[None, :, None, :]
    x1 = x[..., :half].astype(jnp.float32)
    x2 = x[..., half:].astype(jnp.float32)
    return jnp.concatenate([x1 * cos - x2 * sin, x1 * sin + x2 * cos], axis=-1).astype(x.dtype)


def retention(q, k, v, s0):
    bsz, L, H, dk = q.shape
    dv = v.shape[-1]
    C = min(CHUNK, L)
    n = L // C
    log_g = jnp.log(1.0 - 2.0 ** (-5.0 - jnp.arange(H, dtype=jnp.float32)))
    qc = q.reshape(bsz, n, C, H, dk).astype(jnp.float32)
    kc = k.reshape(bsz, n, C, H, dk).astype(jnp.float32)
    vc = v.reshape(bsz, n, C, H, dv).astype(jnp.float32)
    idx = jnp.arange(C, dtype=jnp.float32)
    diff = idx[:, None] - idx[None, :]
    causal = diff >= 0
    decay = jnp.where(causal[None], jnp.exp(log_g[:, None, None] * jnp.where(causal, diff, 0.0)[None]), 0.0)
    scores = jnp.einsum('bnihd,bnjhd->bnhij', qc, kc) * decay
    intra = jnp.einsum('bnhij,bnjhe->bnihe', scores, vc)
    zeta = jnp.exp(log_g[:, None] * (C - 1.0 - idx)[None, :])
    kv = jnp.einsum('bnjhd,hj,bnjhe->bnhde', kc, zeta, vc)
    g_chunk = jnp.exp(log_g * C)[None, :, None, None]

    def step(S, kv_c):
        return g_chunk * S + kv_c, S

    s_fin, s_prev = lax.scan(step, s0.astype(jnp.float32), jnp.moveaxis(kv, 1, 0))
    s_prev = jnp.moveaxis(s_prev, 0, 1)
    xi = jnp.exp(log_g[:, None] * (idx + 1.0)[None, :])
    cross = jnp.einsum('bnihd,bnhde,hi->bnihe', qc, s_prev, xi)
    out = (intra + cross).reshape(bsz, L, H, dv)
    return out.astype(q.dtype), s_fin


def peer(h, w_q, sub_keys, expert_u, expert_v):
    bsz, L, D = h.shape
    T = bsz * L
    blk = math.gcd(T, PEER_BLOCK) if T > PEER_BLOCK else T

    def block_fn(xb):
        q = (xb @ w_q).reshape(xb.shape[0], PEER_HEADS, 2, PEER_DHALF)
        s = jnp.einsum('thpd,pkd->thpk', q, sub_keys).astype(jnp.float32)
        v1, i1 = lax.top_k(s[:, :, 0], PEER_TOPK)
        v2, i2 = lax.top_k(s[:, :, 1], PEER_TOPK)
        comb = (v1[..., :, None] + v2[..., None, :]).reshape(xb.shape[0], PEER_HEADS, PEER_TOPK * PEER_TOPK)
        vals, sel = lax.top_k(comb, PEER_TOPK)
        e1 = jnp.take_along_axis(i1, sel // PEER_TOPK, axis=-1)
        e2 = jnp.take_along_axis(i2, sel % PEER_TOPK, axis=-1)
        eidx = (e1 * PEER_NKEYS + e2).reshape(xb.shape[0], PEER_HEADS * PEER_TOPK)
        gates = jax.nn.softmax(vals, axis=-1).reshape(xb.shape[0], PEER_HEADS * PEER_TOPK)
        u_sel = expert_u[eidx]
        act = jax.nn.gelu(jnp.einsum('td,tkd->tk', xb, u_sel))
        v_sel = expert_v[eidx]
        return jnp.einsum('tk,tkd->td', (gates * act).astype(xb.dtype), v_sel)

    out = lax.map(block_fn, h.reshape(T // blk, blk, D))
    return out.reshape(bsz, L, D)


def trunk(x, c, pool_hist, ret_state, pos0, params):
    (w_ada, b_ada, w_in, w_pool_grp, pool_scale, w_branch_pool, w_branch_ret, w_out,
     ln1_g, ln1_b, w_peer_q, peer_sub_keys, peer_u, peer_v, ln2_g, ln2_b) = params
    bsz, L, _ = x.shape
    pos = pos0 + jnp.arange(L)
    new_hist, new_state = [], []
    for l in range(DEPTH):
        mod = jax.nn.silu(c) @ w_ada[l] + b_ada[l]
        sh1, sc1, g1, sh2, sc2, g2 = jnp.split(mod[:, None, :], 6, axis=-1)
        h = layer_norm(x) * (1 + sc1) + sh1
        proj = h @ w_in[l]
        p = proj[..., :OFF_Q]
        q = proj[..., OFF_Q:OFF_K].reshape(bsz, L, N_RET_HEADS, DK_RET)
        k = proj[..., OFF_K:OFF_V].reshape(bsz, L, N_RET_HEADS, DK_RET)
        v = proj[..., OFF_V:OFF_G].reshape(bsz, L, N_RET_HEADS, DV_RET)
        g_ret = proj[..., OFF_G:OFF_BG]
        bgates = jax.nn.sigmoid(proj[..., OFF_BG:].astype(jnp.float32)).astype(x.dtype)
        pool_out, hist_l = pool_mixer(p, pool_hist[l], pos0, w_pool_grp[l], pool_scale[l])
        ret, s_l = retention(rotary(q, pos), rotary(k, pos) * RET_SCALE, v, ret_state[l])
        ret = layer_norm(ret).reshape(bsz, L, D_RET_V) * jax.nn.silu(g_ret)
        merged = (bgates[..., :D_MODEL] * (pool_out @ w_branch_pool[l])
                  + bgates[..., D_MODEL:] * (ret @ w_branch_ret[l]))
        x = layer_norm(DEEPNORM_ALPHA * x + g1 * (merged @ w_out[l]), ln1_g[l], ln1_b[l])
        h2 = layer_norm(x) * (1 + sc2) + sh2
        y = peer(h2, w_peer_q[l], peer_sub_keys[l], peer_u[l], peer_v[l])
        x = layer_norm(DEEPNORM_ALPHA * x + g2 * y, ln2_g[l], ln2_b[l])
        new_hist.append(hist_l)
        new_state.append(s_l)
    return x, jnp.stack(new_hist), jnp.stack(new_state)


def setup_inputs(seed: int = 0) -> dict:
    key = jax.random.key(seed)
    ks = jax.random.split(key, 24)

    def nrm(k, shape, s):
        return jax.random.normal(k, shape, jnp.float32) * s

    inv_d = D_MODEL ** -0.5
    return {
        'x_prompt': nrm(ks[0], (BATCH, SEQ, D_MODEL), 1.0),
        'x_sample': nrm(ks[1], (DEC_BATCH, DEC_SEQ, D_MODEL), 1.0),
        'cache_pool': nrm(ks[2], (DEPTH, DEC_BATCH, POOL_HIST, D_POOL), 1.0),
        'state_ret': nrm(ks[3], (DEPTH, DEC_BATCH, N_RET_HEADS, DK_RET, DV_RET), 1.0),
        'c_prompt': nrm(ks[4], (BATCH, D_MODEL), 1.0),
        'c_sample': nrm(ks[5], (DEC_BATCH, D_MODEL), 1.0),
        'w_ada': nrm(ks[6], (DEPTH, D_MODEL, 6 * D_MODEL), 0.5 * inv_d),
        'b_ada': nrm(ks[7], (DEPTH, 6 * D_MODEL), 0.01),
        'w_in': nrm(ks[8], (DEPTH, D_MODEL, D_IN), inv_d),
        'w_pool_grp': nrm(ks[9], (DEPTH, N_POOL_GROUPS, POOL_GROUP, POOL_GROUP), POOL_GROUP ** -0.5),
        'pool_scale': 1.0 + nrm(ks[10], (DEPTH, D_POOL), 0.1),
        'w_branch_pool': nrm(ks[11], (DEPTH, D_POOL, D_MODEL), D_POOL ** -0.5),
        'w_branch_ret': nrm(ks[12], (DEPTH, D_RET_V, D_MODEL), D_RET_V ** -0.5),
        'w_out': nrm(ks[13], (DEPTH, D_MODEL, D_MODEL), DEEPNORM_BETA * inv_d),
        'ln1_g': 1.0 + nrm(ks[14], (DEPTH, D_MODEL), 0.05),
        'ln1_b': nrm(ks[15], (DEPTH, D_MODEL), 0.01),
        'w_peer_q': nrm(ks[16], (DEPTH, D_MODEL, PEER_HEADS * PEER_DQ), inv_d),
        'peer_sub_keys': nrm(ks[17], (DEPTH, 2, PEER_NKEYS, PEER_DHALF), PEER_DHALF ** -0.5),
        'peer_u': nrm(ks[18], (DEPTH, PEER_N, D_MODEL), inv_d),
        'peer_v': nrm(ks[19], (DEPTH, PEER_N, D_MODEL), DEEPNORM_BETA),
        'ln2_g': 1.0 + nrm(ks[20], (DEPTH, D_MODEL), 0.05),
        'ln2_b': nrm(ks[21], (DEPTH, D_MODEL), 0.01),
    }


def reference(x_prompt, x_sample, cache_pool, state_ret, c_prompt, c_sample,
              w_ada, b_ada, w_in, w_pool_grp, pool_scale, w_branch_pool, w_branch_ret, w_out,
              ln1_g, ln1_b, w_peer_q, peer_sub_keys, peer_u, peer_v, ln2_g, ln2_b):
    params = (w_ada, b_ada, w_in, w_pool_grp, pool_scale, w_branch_pool, w_branch_ret, w_out,
              ln1_g, ln1_b, w_peer_q, peer_sub_keys, peer_u, peer_v, ln2_g, ln2_b)
    n_prompt = x_prompt.shape[0]
    zero_hist = jnp.zeros((DEPTH, n_prompt, POOL_HIST, D_POOL), x_prompt.dtype)
    zero_state = jnp.zeros((DEPTH, n_prompt, N_RET_HEADS, DK_RET, DV_RET), jnp.float32)
    y_prompt, pool_p, ret_p = trunk(x_prompt, c_prompt, zero_hist, zero_state, 0, params)
    y_sample, pool_s, ret_s = trunk(x_sample, c_sample, cache_pool, state_ret, PAST_LEN, params)
    return (y_prompt, y_sample, pool_p, ret_p.astype(state_ret.dtype), pool_s, ret_s.astype(state_ret.dtype))
```

```python
import functools
import math

import jax
import jax.numpy as jnp
from jax import lax
from jax.experimental import pallas as pl
from jax.experimental.pallas import tpu as pltpu

F32 = jnp.float32
BF16 = jnp.bfloat16

D_MODEL = 1024
DEPTH = 2
PAST_LEN = 2048
POOL_WINDOWS = (2, 4, 8, 16)
D_POOL = 512
POOL_GROUP = 128
POOL_HIST = 15
HIST_ROWS = 16
N_RET_HEADS = 8
DK_RET = 128
D_RET = N_RET_HEADS * DK_RET
RET_SCALE = DK_RET ** -0.5
ROPE_BASE = 10000.0
PEER_HEADS = 8
PEER_NKEYS = 128
PEER_N = PEER_NKEYS * PEER_NKEYS
PEER_DQ = 256
PEER_DHALF = 128
PEER_TOPK = 16
LN_EPS = 1e-5
ALPHA = (2 * DEPTH) ** 0.25
OFF_Q = D_POOL
OFF_K = OFF_Q + D_RET
OFF_V = OFF_K + D_RET
OFF_G = OFF_V + D_RET
OFF_BG = OFF_G + D_RET
D_IN = OFF_BG + 2 * D_MODEL
LOG_G = tuple(math.log(1.0 - 2.0 ** (-5.0 - h)) for h in range(N_RET_HEADS))
LANES = 128
SUBLANES = 8
KEY_TILES = PEER_NKEYS // SUBLANES
VMEM_LIMIT = 56 * 1024 * 1024
NEG_INF = float("-inf")


def _ln(x):
    mu = jnp.mean(x, axis=-1, keepdims=True)
    xc = x - mu
    var = jnp.mean(xc * xc, axis=-1, keepdims=True)
    return xc * lax.rsqrt(var + LN_EPS)


def _gelu_tanh(x):
    inner = math.sqrt(2.0 / math.pi) * (x + 0.044715 * (x * x * x))
    return 0.5 * x * (1.0 + jnp.tanh(inner))


def _params(*sem):
    return pltpu.CompilerParams(dimension_semantics=sem, vmem_limit_bytes=VMEM_LIMIT)


def _ada_kernel(c_ref, w_ref, b_ref, o_ref):
    c = c_ref[...]
    a = (c * jax.nn.sigmoid(c)).astype(BF16)
    o_ref[...] = jnp.dot(a, w_ref[...].astype(BF16), preferred_element_type=F32) + b_ref[...]


def _ada(c_pad, w_ada, b_ada):
    rows = c_pad.shape[0]
    n_out = w_ada.shape[-1]
    ct = 1536
    return pl.pallas_call(
        _ada_kernel,
        grid=(DEPTH, n_out // ct),
        in_specs=[
            pl.BlockSpec((rows, D_MODEL), lambda l, j: (0, 0)),
            pl.BlockSpec((None, D_MODEL, ct), lambda l, j: (l, 0, j)),
            pl.BlockSpec((None, 1, ct), lambda l, j: (l, 0, j)),
        ],
        out_specs=pl.BlockSpec((None, rows, ct), lambda l, j: (l, 0, j)),
        out_shape=jax.ShapeDtypeStruct((DEPTH, rows, n_out), F32),
        compiler_params=_params("arbitrary", "arbitrary"),
        name="ada",
    )(c_pad, w_ada, b_ada.reshape(DEPTH, 1, n_out))


def _mod_spec(per_token, tt, tiles_per_seq):
    if per_token:
        return pl.BlockSpec((1, tt, D_MODEL), lambda i, *_: (0, i, 0))
    return pl.BlockSpec((1, 1, D_MODEL), lambda i, *_: (i // tiles_per_seq, 0, 0))


def _proj_kernel(x_ref, sc_ref, sh_ref, w_ref, o_ref, h_scr):
    @pl.when(pl.program_id(1) == 0)
    def _():
        h = _ln(x_ref[...]) * (1.0 + sc_ref[0]) + sh_ref[0]
        h_scr[...] = h.astype(BF16)

    o_ref[...] = jnp.dot(h_scr[...], w_ref[...], preferred_element_type=F32)


def _proj(x_flat, sc, sh, w_in_b, layer, tt, per_token, tiles_per_seq):
    t = x_flat.shape[0]
    nt = D_IN // 4
    return pl.pallas_call(
        _proj_kernel,
        grid=(t // tt, D_IN // nt),
        in_specs=[
            pl.BlockSpec((tt, D_MODEL), lambda i, j: (i, 0)),
            _mod_spec(per_token, tt, tiles_per_seq),
            _mod_spec(per_token, tt, tiles_per_seq),
            pl.BlockSpec((None, D_MODEL, nt), lambda i, j: (layer, 0, j)),
        ],
        out_specs=pl.BlockSpec((tt, nt), lambda i, j: (i, j)),
        out_shape=jax.ShapeDtypeStruct((t, D_IN), F32),
        scratch_shapes=[pltpu.VMEM((tt, D_MODEL), BF16)],
        compiler_params=_params("arbitrary", "arbitrary"),
        name="proj",
    )(x_flat, sc, sh, w_in_b)


def _mix_kernel(proj_ref, x_ref, g1_ref, cos_ref, sin_ref, hist_ref, s0_ref,
                wgrp_ref, pscale_ref, wbp_ref, wbr_ref, wout_ref, lng_ref, lnb_ref,
                x1_ref, sfin_ref,
                state_scr, ext_scr, decay_scr, xi_scr, zeta_scr, retg_scr, pool_scr,
                *, tt, pos0, n_tiles):
    j = pl.program_id(1)

    @pl.when(j == 0)
    def _init():
        state_scr[...] = s0_ref[0]
        ext_scr[0:HIST_ROWS, :] = hist_ref[0]
        row = lax.broadcasted_iota(jnp.int32, (tt, tt), 0)
        col = lax.broadcasted_iota(jnp.int32, (tt, tt), 1)
        causal = row >= col
        diff = jnp.where(causal, (row - col).astype(F32), 0.0)
        rowl = lax.broadcasted_iota(jnp.int32, (tt, LANES), 0).astype(F32)
        for h in range(N_RET_HEADS):
            decay_scr[h] = jnp.where(causal, jnp.exp(LOG_G[h] * diff), 0.0)
            xi_scr[h] = jnp.exp(LOG_G[h] * (rowl + 1.0))
            zeta_scr[h] = jnp.exp(LOG_G[h] * (tt - 1.0 - rowl))

    p = proj_ref[0, :, 0:D_POOL]
    ext_scr[HIST_ROWS:HIST_ROWS + tt, :] = p
    pos = pos0 + j * tt + lax.broadcasted_iota(jnp.int32, (tt, POOL_GROUP), 0)
    for g, w in enumerate(POOL_WINDOWS):
        cs = slice(g * POOL_GROUP, (g + 1) * POOL_GROUP)
        ws = ext_scr[HIST_ROWS:HIST_ROWS + tt, cs]
        for d in range(1, w):
            ws = ws + ext_scr[HIST_ROWS - d:HIST_ROWS - d + tt, cs]
        cnt = jnp.minimum(pos + 1, w).astype(F32)
        pooled = ws / cnt - p[:, cs]
        mixed = jnp.dot(pooled.astype(BF16), wgrp_ref[g], preferred_element_type=F32)
        pool_scr[:, cs] = (mixed * pscale_ref[:, cs]).astype(BF16)
    ext_scr[0:HIST_ROWS, :] = ext_scr[tt:tt + HIST_ROWS, :]

    cosv = cos_ref[...]
    sinv = sin_ref[...]
    for h in range(N_RET_HEADS):
        hs = slice(h * DK_RET, (h + 1) * DK_RET)
        q = proj_ref[0, :, OFF_Q + h * DK_RET:OFF_Q + (h + 1) * DK_RET]
        k = proj_ref[0, :, OFF_K + h * DK_RET:OFF_K + (h + 1) * DK_RET]
        v = proj_ref[0, :, OFF_V + h * DK_RET:OFF_V + (h + 1) * DK_RET].astype(BF16)
        g = proj_ref[0, :, OFF_G + h * DK_RET:OFF_G + (h + 1) * DK_RET]
        qr = q * cosv + pltpu.roll(q, DK_RET // 2, 1) * sinv
        kr = (k * cosv + pltpu.roll(k, DK_RET // 2, 1) * sinv) * RET_SCALE
        qb = qr.astype(BF16)
        scores = lax.dot_general(qb, kr.astype(BF16), (((1,), (1,)), ((), ())),
                                 preferred_element_type=F32) * decay_scr[h]
        intra = jnp.dot(scores.astype(BF16), v, preferred_element_type=F32)
        s_prev = state_scr[h]
        cross = jnp.dot(qb, s_prev.astype(BF16), preferred_element_type=F32) * xi_scr[h]
        kz = (kr * zeta_scr[h]).astype(BF16)
        kv = lax.dot_general(kz, v, (((0,), (0,)), ((), ())), preferred_element_type=F32)
        state_scr[h] = math.exp(LOG_G[h] * tt) * s_prev + kv
        retg_scr[:, hs] = (_ln(intra + cross) * (g * jax.nn.sigmoid(g))).astype(BF16)

    bgp = jax.nn.sigmoid(proj_ref[0, :, OFF_BG:OFF_BG + D_MODEL])
    bgr = jax.nn.sigmoid(proj_ref[0, :, OFF_BG + D_MODEL:OFF_BG + 2 * D_MODEL])
    pb = jnp.dot(pool_scr[...], wbp_ref[...], preferred_element_type=F32)
    rb = jnp.dot(retg_scr[...], wbr_ref[...], preferred_element_type=F32)
    merged = (bgp * pb + bgr * rb).astype(BF16)
    z = jnp.dot(merged, wout_ref[...], preferred_element_type=F32)
    x1_ref[0] = _ln(ALPHA * x_ref[0] + g1_ref[0] * z) * lng_ref[...] + lnb_ref[...]

    @pl.when(j == n_tiles - 1)
    def _fin():
        sfin_ref[0] = state_scr[...]


def _mix(proj, x, g1, cos2, sin2, hist16, s0, wts, layer, tt, pos0):
    bsz, seq, _ = x.shape
    n_tiles = seq // tt
    wgrp_b, pscale, wbp_b, wbr_b, wout_b, ln1_g, ln1_b = wts
    const2 = lambda b, j: (layer, 0, 0)
    return pl.pallas_call(
        functools.partial(_mix_kernel, tt=tt, pos0=pos0, n_tiles=n_tiles),
        grid=(bsz, n_tiles),
        in_specs=[
            pl.BlockSpec((1, tt, D_IN), lambda b, j: (b, j, 0)),
            pl.BlockSpec((1, tt, D_MODEL), lambda b, j: (b, j, 0)),
            pl.BlockSpec((1, 1, D_MODEL), lambda b, j: (b, 0, 0)),
            pl.BlockSpec((tt, LANES), lambda b, j: (j, 0)),
            pl.BlockSpec((tt, LANES), lambda b, j: (j, 0)),
            pl.BlockSpec((1, HIST_ROWS, D_POOL), lambda b, j: (b, 0, 0)),
            pl.BlockSpec((1, N_RET_HEADS, DK_RET, DK_RET), lambda b, j: (b, 0, 0, 0)),
            pl.BlockSpec((None, len(POOL_WINDOWS), POOL_GROUP, POOL_GROUP),
                         lambda b, j: (layer, 0, 0, 0)),
            pl.BlockSpec((None, 1, D_POOL), const2),
            pl.BlockSpec((None, D_POOL, D_MODEL), const2),
            pl.BlockSpec((None, D_RET, D_MODEL), const2),
            pl.BlockSpec((None, D_MODEL, D_MODEL), const2),
            pl.BlockSpec((None, 1, D_MODEL), const2),
            pl.BlockSpec((None, 1, D_MODEL), const2),
        ],
        out_specs=[
            pl.BlockSpec((1, tt, D_MODEL), lambda b, j: (b, j, 0)),
            pl.BlockSpec((1, N_RET_HEADS, DK_RET, DK_RET), lambda b, j: (b, 0, 0, 0)),
        ],
        out_shape=[
            jax.ShapeDtypeStruct((bsz, seq, D_MODEL), F32),
            jax.ShapeDtypeStruct((bsz, N_RET_HEADS, DK_RET, DK_RET), F32),
        ],
        scratch_shapes=[
            pltpu.VMEM((N_RET_HEADS, DK_RET, DK_RET), F32),
            pltpu.VMEM((HIST_ROWS + tt, D_POOL), F32),
            pltpu.VMEM((N_RET_HEADS, tt, tt), F32),
            pltpu.VMEM((N_RET_HEADS, tt, LANES), F32),
            pltpu.VMEM((N_RET_HEADS, tt, LANES), F32),
            pltpu.VMEM((tt, D_RET), BF16),
            pltpu.VMEM((tt, D_POOL), BF16),
        ],
        compiler_params=_params("arbitrary", "arbitrary"),
        name="mix",
    )(proj, x, g1, cos2, sin2, hist16, s0, wgrp_b, pscale, wbp_b, wbr_b, wout_b, ln1_g, ln1_b)


def _top16_rows(s):
    rows = []
    cur = s
    for i in range(PEER_TOPK):
        m = jnp.max(cur, axis=0, keepdims=True)
        rows.append(m)
        if i + 1 < PEER_TOPK:
            cur = jnp.where(cur == m, NEG_INF, cur)
    return rows


def _stack_rows(rows, tt):
    ridx = lax.broadcasted_iota(jnp.int32, (PEER_TOPK, tt), 0)
    out = jnp.zeros((PEER_TOPK, tt), F32)
    for i, r in enumerate(rows):
        out = jnp.where(ridx == i, r, out)
    return out


def _peer_kernel(x1_ref, sc_ref, sh_ref, g2_ref, wqt_ref, keys_ref, u_ref, vt_ref,
                 lng_ref, lnb_ref, o_ref,
                 h2_scr, qt_scr, c_scr, e1_scr, s2_scr, e2_scr, cb_scr, eb_scr, s_scr, ga_scr, acc_scr,
                 *, tt, na, n_chunks):
    e = pl.program_id(1)

    @pl.when(e == 0)
    def _prep():
        h2 = (_ln(x1_ref[...]) * (1.0 + sc_ref[0]) + sh_ref[0]).astype(BF16)
        h2_scr[...] = h2
        qt = lax.dot_general(wqt_ref[...], h2, (((1,), (1,)), ((), ())),
                             preferred_element_type=F32)
        qt_scr[...] = qt.astype(BF16)

        def head(h, carry):
            base = pl.multiple_of(h * PEER_DQ, PEER_DQ)
            s1 = jnp.dot(keys_ref[0], qt_scr[pl.ds(base, PEER_DHALF), :],
                         preferred_element_type=F32)
            s2 = jnp.dot(keys_ref[1], qt_scr[pl.ds(base + PEER_DHALF, PEER_DHALF), :],
                         preferred_element_type=F32)
            r1 = _top16_rows(s1)
            r2 = _top16_rows(s2)
            v1 = _stack_rows(r1, tt)
            v2 = _stack_rows(r2, tt)
            ridx = lax.broadcasted_iota(jnp.int32, (PEER_TOPK, tt), 0)
            half = PEER_TOPK // 2
            cand = [jnp.where(ridx < PEER_TOPK // (i + 1), r1[i] + v2, NEG_INF)
                    for i in range(half)]
            cand.append(jnp.where(ridx >= half, v1 + r2[0], NEG_INF))
            cur = cand
            tau = None
            for it in range(PEER_TOPK):
                mx = functools.reduce(jnp.maximum, cur)
                tau = jnp.max(mx, axis=0, keepdims=True)
                if it + 1 < PEER_TOPK:
                    cur = [jnp.where(c == tau, NEG_INF, c) for c in cur]
            top = r1[0] + r2[0]
            zsum = jnp.zeros((PEER_TOPK, tt), F32)
            for c in cand:
                zsum = zsum + jnp.where(c >= tau, jnp.exp(c - top), 0.0)
            z = jnp.sum(zsum, axis=0, keepdims=True)
            c_scr[h] = tau - s1
            e1_scr[h] = jnp.exp(s1 - r1[0])
            s2_scr[h] = s2.reshape(KEY_TILES, SUBLANES, tt)
            e2_scr[h] = (jnp.exp(s2 - r2[0]) / z).reshape(KEY_TILES, SUBLANES, tt)
            return carry

        lax.fori_loop(0, PEER_HEADS, head, 0)

    s_scr[...] = lax.dot_general(u_ref[...], h2_scr[...], (((1,), (1,)), ((), ())),
                                 preferred_element_type=F32)

    a0 = pl.multiple_of(e * na, na)
    for h in range(PEER_HEADS):
        cslab = c_scr[h, pl.ds(a0, na), :]
        eslab = e1_scr[h, pl.ds(a0, na), :]
        for r in range(na):
            cb_scr[h, r] = jnp.broadcast_to(cslab[r:r + 1, :], (SUBLANES, tt))
            eb_scr[h, r] = jnp.broadcast_to(eslab[r:r + 1, :], (SUBLANES, tt))

    def key_row(ai, carry):
        r0 = pl.multiple_of(ai * PEER_NKEYS, PEER_NKEYS)
        for lg in range(tt // LANES):
            ls = slice(lg * LANES, (lg + 1) * LANES)
            gate = jnp.zeros((KEY_TILES, SUBLANES, LANES), F32)
            for h in range(PEER_HEADS):
                crow = cb_scr[h, ai, :, ls][None]
                erow = eb_scr[h, ai, :, ls][None]
                gate = gate + jnp.where(s2_scr[h, :, :, ls] >= crow, e2_scr[h, :, :, ls], 0.0) * erow
            act = _gelu_tanh(s_scr[pl.ds(r0, PEER_NKEYS), ls])
            ga = gate.reshape(PEER_NKEYS, LANES) * act
            ga_scr[pl.ds(r0, PEER_NKEYS), ls] = ga.astype(BF16)
        return carry

    lax.fori_loop(0, na, key_row, 0)

    contrib = jnp.dot(vt_ref[...], ga_scr[...], preferred_element_type=F32)

    @pl.when(e == 0)
    def _():
        acc_scr[...] = contrib

    @pl.when(e > 0)
    def _():
        acc_scr[...] += contrib

    @pl.when(e == n_chunks - 1)
    def _fin():
        y = acc_scr[...].T
        o_ref[...] = _ln(ALPHA * x1_ref[...] + g2_ref[0] * y) * lng_ref[...] + lnb_ref[...]


def _peer(x1_flat, sc, sh, g2, wts, layer, tt, per_token, tiles_per_seq):
    t = x1_flat.shape[0]
    wqt_b, keys_b, u_b, vt_b, ln2_g, ln2_b = wts
    na = 8
    ec = na * PEER_NKEYS
    n_chunks = PEER_N // ec
    mod = _mod_spec(per_token, tt, tiles_per_seq)
    return pl.pallas_call(
        functools.partial(_peer_kernel, tt=tt, na=na, n_chunks=n_chunks),
        grid=(t // tt, n_chunks),
        in_specs=[
            pl.BlockSpec((tt, D_MODEL), lambda i, e: (i, 0)),
            mod, mod, mod,
            pl.BlockSpec((None, PEER_HEADS * PEER_DQ, D_MODEL), lambda i, e: (layer, 0, 0)),
            pl.BlockSpec((None, 2, PEER_NKEYS, PEER_DHALF), lambda i, e: (layer, 0, 0, 0)),
            pl.BlockSpec((None, ec, D_MODEL), lambda i, e: (layer, e, 0)),
            pl.BlockSpec((None, D_MODEL, ec), lambda i, e: (layer, 0, e)),
            pl.BlockSpec((None, 1, D_MODEL), lambda i, e: (layer, 0, 0)),
            pl.BlockSpec((None, 1, D_MODEL), lambda i, e: (layer, 0, 0)),
        ],
        out_specs=pl.BlockSpec((tt, D_MODEL), lambda i, e: (i, 0)),
        out_shape=jax.ShapeDtypeStruct((t, D_MODEL), F32),
        scratch_shapes=[
            pltpu.VMEM((tt, D_MODEL), BF16),
            pltpu.VMEM((PEER_HEADS * PEER_DQ, tt), BF16),
            pltpu.VMEM((PEER_HEADS, PEER_NKEYS, tt), F32),
            pltpu.VMEM((PEER_HEADS, PEER_NKEYS, tt), F32),
            pltpu.VMEM((PEER_HEADS, KEY_TILES, SUBLANES, tt), F32),
            pltpu.VMEM((PEER_HEADS, KEY_TILES, SUBLANES, tt), F32),
            pltpu.VMEM((PEER_HEADS, na, SUBLANES, tt), F32),
            pltpu.VMEM((PEER_HEADS, na, SUBLANES, tt), F32),
            pltpu.VMEM((ec, tt), F32),
            pltpu.VMEM((ec, tt), BF16),
            pltpu.VMEM((D_MODEL, tt), F32),
        ],
        compiler_params=_params("arbitrary", "arbitrary"),
        name="peer",
    )(x1_flat, sc, sh, g2, wqt_b, keys_b, u_b, vt_b, ln2_g, ln2_b)


def _rope_tables(pos0, seq):
    half = DK_RET // 2
    inv = ROPE_BASE ** (-jnp.arange(half, dtype=F32) / half)
    ang = (pos0 + jnp.arange(seq)).astype(F32)[:, None] * inv[None, :]
    cos = jnp.cos(ang)
    sin = jnp.sin(ang)
    return jnp.concatenate([cos, cos], axis=-1), jnp.concatenate([-sin, sin], axis=-1)


def _trunk(x, mods, pool_hist, ret_state, pos0, wts, tiles):
    bsz, seq, _ = x.shape
    tt_proj, tt_mix, tt_peer = tiles
    per_token = seq < tt_peer
    cos2, sin2 = _rope_tables(pos0, seq)
    hist16 = jnp.pad(pool_hist, ((0, 0), (0, 0), (HIST_ROWS - POOL_HIST, 0), (0, 0)))
    new_hist, new_state = [], []
    for l in range(DEPTH):
        chunks = [mods[l, :, k * D_MODEL:(k + 1) * D_MODEL] for k in range(6)]
        seq_rows = [c[:, None, :] for c in chunks]
        if per_token:
            tok_rows = [jnp.repeat(c, seq, axis=0)[None] for c in chunks]
        else:
            tok_rows = seq_rows
        sh1, sc1, _, sh2, sc2, g2 = tok_rows
        g1 = seq_rows[2]
        proj = _proj(x.reshape(bsz * seq, D_MODEL), sc1, sh1, wts["w_in"], l,
                     tt_proj, per_token, seq // tt_proj if not per_token else 1)
        proj = proj.reshape(bsz, seq, D_IN)
        x1, s_l = _mix(proj, x, g1, cos2, sin2, hist16[l], ret_state[l],
                       (wts["w_grp"], wts["pool_scale"], wts["w_bp"], wts["w_br"], wts["w_out"],
                        wts["ln1_g"], wts["ln1_b"]), l, tt_mix, pos0)
        x2 = _peer(x1.reshape(bsz * seq, D_MODEL), sc2, sh2, g2,
                   (wts["w_qt"], wts["keys"], wts["u"], wts["vt"], wts["ln2_g"], wts["ln2_b"]),
                   l, tt_peer, per_token, seq // tt_peer if not per_token else 1)
        x = x2.reshape(bsz, seq, D_MODEL)
        new_hist.append(proj[:, seq - POOL_HIST:, :D_POOL])
        new_state.append(s_l)
    return x, jnp.stack(new_hist), jnp.stack(new_state)


def kernel(x_prompt, x_sample, cache_pool, state_ret, c_prompt, c_sample, w_ada, b_ada, w_in,
           w_pool_grp, pool_scale, w_branch_pool, w_branch_ret, w_out, ln1_g, ln1_b, w_peer_q,
           peer_sub_keys, peer_u, peer_v, ln2_g, ln2_b):
    n_prompt = x_prompt.shape[0]
    n_sample = x_sample.shape[0]
    wts = {
        "w_in": w_in.astype(BF16),
        "w_grp": w_pool_grp.astype(BF16),
        "pool_scale": pool_scale[:, None, :],
        "w_bp": w_branch_pool.astype(BF16),
        "w_br": w_branch_ret.astype(BF16),
        "w_out": w_out.astype(BF16),
        "ln1_g": ln1_g[:, None, :],
        "ln1_b": ln1_b[:, None, :],
        "w_qt": jnp.swapaxes(w_peer_q, 1, 2).astype(BF16),
        "keys": peer_sub_keys.astype(BF16),
        "u": peer_u.astype(BF16),
        "vt": jnp.swapaxes(peer_v, 1, 2).astype(BF16),
        "ln2_g": ln2_g[:, None, :],
        "ln2_b": ln2_b[:, None, :],
    }
    n_seq = n_prompt + n_sample
    c_pad = jnp.pad(jnp.concatenate([c_prompt, c_sample], axis=0), ((0, 16 - n_seq), (0, 0)))
    mods = _ada(c_pad, w_ada, b_ada)
    zero_hist = jnp.zeros((DEPTH, n_prompt, POOL_HIST, D_POOL), x_prompt.dtype)
    zero_state = jnp.zeros((DEPTH, n_prompt, N_RET_HEADS, DK_RET, DK_RET), F32)
    y_p, pool_p, ret_p = _trunk(x_prompt, mods[:, :n_prompt], zero_hist, zero_state, 0, wts,
                                (1024, 256, 512))
    y_s, pool_s, ret_s = _trunk(x_sample, mods[:, n_prompt:n_seq], cache_pool, state_ret,
                                PAST_LEN, wts, (256, 32, 256))
    return (y_p, y_s, pool_p, ret_p.astype(state_ret.dtype), pool_s, ret_s.astype(state_ret.dtype))
```

```python
import functools
import math

import jax
import jax.numpy as jnp
from jax import lax
from jax.experimental import pallas as pl
from jax.experimental.pallas import tpu as pltpu

F32 = jnp.float32
BF16 = jnp.bfloat16

D_MODEL = 1024
DEPTH = 2
PAST_LEN = 2048
POOL_WINDOWS = (2, 4, 8, 16)
D_POOL = 512
POOL_GROUP = 128
POOL_HIST = 15
HIST_ROWS = 16
N_RET_HEADS = 8
DK_RET = 128
D_RET = N_RET_HEADS * DK_RET
RET_SCALE = DK_RET ** -0.5
ROPE_BASE = 10000.0
PEER_HEADS = 8
PEER_NKEYS = 128
PEER_N = PEER_NKEYS * PEER_NKEYS
PEER_DQ = 256
PEER_DHALF = 128
PEER_TOPK = 16
LN_EPS = 1e-5
ALPHA = (2 * DEPTH) ** 0.25
OFF_Q = D_POOL
OFF_K = OFF_Q + D_RET
OFF_V = OFF_K + D_RET
OFF_G = OFF_V + D_RET
OFF_BG = OFF_G + D_RET
D_IN = OFF_BG + 2 * D_MODEL
LOG_G = tuple(math.log(1.0 - 2.0 ** (-5.0 - h)) for h in range(N_RET_HEADS))
LANES = 128
SUBLANES = 8
PACKED_ROWS = 16
PACKED_TILES = PEER_NKEYS // PACKED_ROWS
ACT_PIECES = 4
VMEM_LIMIT = 56 * 1024 * 1024
NEG_INF = float("-inf")


def _ln(x):
    mu = jnp.mean(x, axis=-1, keepdims=True)
    xc = x - mu
    var = jnp.mean(xc * xc, axis=-1, keepdims=True)
    return xc * lax.rsqrt(var + LN_EPS)


def _gelu_tanh(x):
    inner = math.sqrt(2.0 / math.pi) * (x + 0.044715 * (x * x * x))
    return 0.5 * x * (1.0 + jnp.tanh(inner))


def _params(*sem):
    return pltpu.CompilerParams(dimension_semantics=sem, vmem_limit_bytes=VMEM_LIMIT)


def _ada_kernel(c_ref, w_ref, b_ref, o_ref):
    c = c_ref[...]
    a = (c * jax.nn.sigmoid(c)).astype(BF16)
    o_ref[...] = jnp.dot(a, w_ref[...].astype(BF16), preferred_element_type=F32) + b_ref[...]


def _ada(c_pad, w_ada, b_ada):
    rows = c_pad.shape[0]
    n_out = w_ada.shape[-1]
    ct = 1536
    return pl.pallas_call(
        _ada_kernel,
        grid=(DEPTH, n_out // ct),
        in_specs=[
            pl.BlockSpec((rows, D_MODEL), lambda l, j: (0, 0)),
            pl.BlockSpec((None, D_MODEL, ct), lambda l, j: (l, 0, j)),
            pl.BlockSpec((None, 1, ct), lambda l, j: (l, 0, j)),
        ],
        out_specs=pl.BlockSpec((None, rows, ct), lambda l, j: (l, 0, j)),
        out_shape=jax.ShapeDtypeStruct((DEPTH, rows, n_out), F32),
        compiler_params=_params("arbitrary", "arbitrary"),
        name="ada",
    )(c_pad, w_ada, b_ada.reshape(DEPTH, 1, n_out))


def _mod_spec(per_token, tt, tiles_per_seq):
    if per_token:
        return pl.BlockSpec((1, tt, D_MODEL), lambda i, *_: (0, i, 0))
    return pl.BlockSpec((1, 1, D_MODEL), lambda i, *_: (i // tiles_per_seq, 0, 0))


def _proj_kernel(x_ref, sc_ref, sh_ref, w_ref, o_ref, h_scr):
    @pl.when(pl.program_id(1) == 0)
    def _():
        h = _ln(x_ref[...]) * (1.0 + sc_ref[0]) + sh_ref[0]
        h_scr[...] = h.astype(BF16)

    o_ref[...] = jnp.dot(h_scr[...], w_ref[...], preferred_element_type=F32)


def _proj(x_flat, sc, sh, w_in_b, layer, tt, per_token, tiles_per_seq):
    t = x_flat.shape[0]
    nt = D_IN // 4
    return pl.pallas_call(
        _proj_kernel,
        grid=(t // tt, D_IN // nt),
        in_specs=[
            pl.BlockSpec((tt, D_MODEL), lambda i, j: (i, 0)),
            _mod_spec(per_token, tt, tiles_per_seq),
            _mod_spec(per_token, tt, tiles_per_seq),
            pl.BlockSpec((None, D_MODEL, nt), lambda i, j: (layer, 0, j)),
        ],
        out_specs=pl.BlockSpec((tt, nt), lambda i, j: (i, j)),
        out_shape=jax.ShapeDtypeStruct((t, D_IN), F32),
        scratch_shapes=[pltpu.VMEM((tt, D_MODEL), BF16)],
        compiler_params=_params("arbitrary", "arbitrary"),
        name="proj",
    )(x_flat, sc, sh, w_in_b)


def _mix_kernel(proj_ref, x_ref, g1_ref, cos_ref, sin_ref, hist_ref, s0_ref,
                wgrp_ref, pscale_ref, wbp_ref, wbr_ref, wout_ref, lng_ref, lnb_ref,
                x1_ref, sfin_ref,
                state_scr, ext_scr, decay_scr, xi_scr, zeta_scr, retg_scr, pool_scr,
                *, tt, pos0, n_tiles):
    j = pl.program_id(1)

    @pl.when(j == 0)
    def _init():
        state_scr[...] = s0_ref[0]
        ext_scr[0:HIST_ROWS, :] = hist_ref[0]
        row = lax.broadcasted_iota(jnp.int32, (tt, tt), 0)
        col = lax.broadcasted_iota(jnp.int32, (tt, tt), 1)
        causal = row >= col
        diff = jnp.where(causal, (row - col).astype(F32), 0.0)
        rowl = lax.broadcasted_iota(jnp.int32, (tt, LANES), 0).astype(F32)
        for h in range(N_RET_HEADS):
            decay_scr[h] = jnp.where(causal, jnp.exp(LOG_G[h] * diff), 0.0)
            xi_scr[h] = jnp.exp(LOG_G[h] * (rowl + 1.0))
            zeta_scr[h] = jnp.exp(LOG_G[h] * (tt - 1.0 - rowl))

    p = proj_ref[0, :, 0:D_POOL]
    ext_scr[HIST_ROWS:HIST_ROWS + tt, :] = p
    pos = pos0 + j * tt + lax.broadcasted_iota(jnp.int32, (tt, POOL_GROUP), 0)
    for g, w in enumerate(POOL_WINDOWS):
        cs = slice(g * POOL_GROUP, (g + 1) * POOL_GROUP)
        ws = ext_scr[HIST_ROWS:HIST_ROWS + tt, cs]
        for d in range(1, w):
            ws = ws + ext_scr[HIST_ROWS - d:HIST_ROWS - d + tt, cs]
        cnt = jnp.minimum(pos + 1, w).astype(F32)
        pooled = ws / cnt - p[:, cs]
        mixed = jnp.dot(pooled.astype(BF16), wgrp_ref[g], preferred_element_type=F32)
        pool_scr[:, cs] = (mixed * pscale_ref[:, cs]).astype(BF16)
    ext_scr[0:HIST_ROWS, :] = ext_scr[tt:tt + HIST_ROWS, :]

    cosv = cos_ref[...]
    sinv = sin_ref[...]
    for h in range(N_RET_HEADS):
        hs = slice(h * DK_RET, (h + 1) * DK_RET)
        q = proj_ref[0, :, OFF_Q + h * DK_RET:OFF_Q + (h + 1) * DK_RET]
        k = proj_ref[0, :, OFF_K + h * DK_RET:OFF_K + (h + 1) * DK_RET]
        v = proj_ref[0, :, OFF_V + h * DK_RET:OFF_V + (h + 1) * DK_RET].astype(BF16)
        g = proj_ref[0, :, OFF_G + h * DK_RET:OFF_G + (h + 1) * DK_RET]
        qr = q * cosv + pltpu.roll(q, DK_RET // 2, 1) * sinv
        kr = (k * cosv + pltpu.roll(k, DK_RET // 2, 1) * sinv) * RET_SCALE
        qb = qr.astype(BF16)
        scores = lax.dot_general(qb, kr.astype(BF16), (((1,), (1,)), ((), ())),
                                 preferred_element_type=F32) * decay_scr[h]
        intra = jnp.dot(scores.astype(BF16), v, preferred_element_type=F32)
        s_prev = state_scr[h]
        cross = jnp.dot(qb, s_prev.astype(BF16), preferred_element_type=F32) * xi_scr[h]
        kz = (kr * zeta_scr[h]).astype(BF16)
        kv = lax.dot_general(kz, v, (((0,), (0,)), ((), ())), preferred_element_type=F32)
        state_scr[h] = math.exp(LOG_G[h] * tt) * s_prev + kv
        retg_scr[:, hs] = (_ln(intra + cross) * (g * jax.nn.sigmoid(g))).astype(BF16)

    bgp = jax.nn.sigmoid(proj_ref[0, :, OFF_BG:OFF_BG + D_MODEL])
    bgr = jax.nn.sigmoid(proj_ref[0, :, OFF_BG + D_MODEL:OFF_BG + 2 * D_MODEL])
    pb = jnp.dot(pool_scr[...], wbp_ref[...], preferred_element_type=F32)
    rb = jnp.dot(retg_scr[...], wbr_ref[...], preferred_element_type=F32)
    merged = (bgp * pb + bgr * rb).astype(BF16)
    z = jnp.dot(merged, wout_ref[...], preferred_element_type=F32)
    x1_ref[0] = _ln(ALPHA * x_ref[0] + g1_ref[0] * z) * lng_ref[...] + lnb_ref[...]

    @pl.when(j == n_tiles - 1)
    def _fin():
        sfin_ref[0] = state_scr[...]


def _mix(proj, x, g1, cos2, sin2, hist16, s0, wts, layer, tt, pos0):
    bsz, seq, _ = x.shape
    n_tiles = seq // tt
    wgrp_b, pscale, wbp_b, wbr_b, wout_b, ln1_g, ln1_b = wts
    const2 = lambda b, j: (layer, 0, 0)
    return pl.pallas_call(
        functools.partial(_mix_kernel, tt=tt, pos0=pos0, n_tiles=n_tiles),
        grid=(bsz, n_tiles),
        in_specs=[
            pl.BlockSpec((1, tt, D_IN), lambda b, j: (b, j, 0)),
            pl.BlockSpec((1, tt, D_MODEL), lambda b, j: (b, j, 0)),
            pl.BlockSpec((1, 1, D_MODEL), lambda b, j: (b, 0, 0)),
            pl.BlockSpec((tt, LANES), lambda b, j: (j, 0)),
            pl.BlockSpec((tt, LANES), lambda b, j: (j, 0)),
            pl.BlockSpec((1, HIST_ROWS, D_POOL), lambda b, j: (b, 0, 0)),
            pl.BlockSpec((1, N_RET_HEADS, DK_RET, DK_RET), lambda b, j: (b, 0, 0, 0)),
            pl.BlockSpec((None, len(POOL_WINDOWS), POOL_GROUP, POOL_GROUP),
                         lambda b, j: (layer, 0, 0, 0)),
            pl.BlockSpec((None, 1, D_POOL), const2),
            pl.BlockSpec((None, D_POOL, D_MODEL), const2),
            pl.BlockSpec((None, D_RET, D_MODEL), const2),
            pl.BlockSpec((None, D_MODEL, D_MODEL), const2),
            pl.BlockSpec((None, 1, D_MODEL), const2),
            pl.BlockSpec((None, 1, D_MODEL), const2),
        ],
        out_specs=[
            pl.BlockSpec((1, tt, D_MODEL), lambda b, j: (b, j, 0)),
            pl.BlockSpec((1, N_RET_HEADS, DK_RET, DK_RET), lambda b, j: (b, 0, 0, 0)),
        ],
        out_shape=[
            jax.ShapeDtypeStruct((bsz, seq, D_MODEL), F32),
            jax.ShapeDtypeStruct((bsz, N_RET_HEADS, DK_RET, DK_RET), F32),
        ],
        scratch_shapes=[
            pltpu.VMEM((N_RET_HEADS, DK_RET, DK_RET), F32),
            pltpu.VMEM((HIST_ROWS + tt, D_POOL), F32),
            pltpu.VMEM((N_RET_HEADS, tt, tt), F32),
            pltpu.VMEM((N_RET_HEADS, tt, LANES), F32),
            pltpu.VMEM((N_RET_HEADS, tt, LANES), F32),
            pltpu.VMEM((tt, D_RET), BF16),
            pltpu.VMEM((tt, D_POOL), BF16),
        ],
        compiler_params=_params("arbitrary", "arbitrary"),
        name="mix",
    )(proj, x, g1, cos2, sin2, hist16, s0, wgrp_b, pscale, wbp_b, wbr_b, wout_b, ln1_g, ln1_b)


def _top16_rows(s, with_rank=False):
    rows = []
    cur = s
    rank = jnp.full(s.shape, float(PEER_TOPK), F32) if with_rank else None
    for i in range(PEER_TOPK):
        m = jnp.max(cur, axis=0, keepdims=True)
        rows.append(m)
        hit = cur == m
        if with_rank:
            rank = jnp.where(hit, float(i), rank)
        if i + 1 < PEER_TOPK:
            cur = jnp.where(hit, NEG_INF, cur)
    return (rows, rank) if with_rank else rows


def _stack_rows(rows, tt):
    ridx = lax.broadcasted_iota(jnp.int32, (PEER_TOPK, tt), 0)
    out = jnp.zeros((PEER_TOPK, tt), F32)
    for i, r in enumerate(rows):
        out = jnp.where(ridx == i, r, out)
    return out


def _activations(u_rows, h2_scr, s_dst, rows, n_lg):
    s_val = lax.dot_general(u_rows, h2_scr[...], (((1,), (1,)), ((), ())),
                            preferred_element_type=F32)
    for lg in range(n_lg):
        s_dst[lg, rows, :] = s_val[:, lg * LANES:(lg + 1) * LANES]


def _peer_kernel(x1_ref, sc_ref, sh_ref, g2_ref, wqt_ref, keys_ref, u_ref, vt_ref,
                 lng_ref, lnb_ref, o_ref,
                 h2_scr, qt_scr, n_scr, e1_scr, r2_scr, e2_scr, nb_scr, eb_scr, s_a, s_b, ga_scr, acc_scr,
                 *, tt, na, n_chunks):
    e = pl.program_id(1)
    n_lg = tt // LANES

    @pl.when(e == 0)
    def _prep():
        h2 = (_ln(x1_ref[...]) * (1.0 + sc_ref[0]) + sh_ref[0]).astype(BF16)
        h2_scr[...] = h2
        qt = lax.dot_general(wqt_ref[...], h2, (((1,), (1,)), ((), ())),
                             preferred_element_type=F32)
        qt_scr[...] = qt.astype(BF16)

        def head(h, carry):
            base = pl.multiple_of(h * PEER_DQ, PEER_DQ)
            s1 = jnp.dot(keys_ref[0], qt_scr[pl.ds(base, PEER_DHALF), :],
                         preferred_element_type=F32)
            s2 = jnp.dot(keys_ref[1], qt_scr[pl.ds(base + PEER_DHALF, PEER_DHALF), :],
                         preferred_element_type=F32)
            r1 = _top16_rows(s1)
            r2, rank2 = _top16_rows(s2, with_rank=True)
            v1 = _stack_rows(r1, tt)
            v2 = _stack_rows(r2, tt)
            ridx = lax.broadcasted_iota(jnp.int32, (PEER_TOPK, tt), 0)
            half = PEER_TOPK // 2
            cand = [jnp.where(ridx < PEER_TOPK // (i + 1), r1[i] + v2, NEG_INF)
                    for i in range(half)]
            cand.append(jnp.where(ridx >= half, v1 + r2[0], NEG_INF))
            cur = cand
            tau = None
            for it in range(PEER_TOPK):
                mx = functools.reduce(jnp.maximum, cur)
                tau = jnp.max(mx, axis=0, keepdims=True)
                if it + 1 < PEER_TOPK:
                    cur = [jnp.where(c == tau, NEG_INF, c) for c in cur]
            top = r1[0] + r2[0]
            zsum = jnp.zeros((PEER_TOPK, tt), F32)
            for c in cand:
                zsum = zsum + jnp.where(c >= tau, jnp.exp(c - top), 0.0)
            z = jnp.sum(zsum, axis=0, keepdims=True)
            c = tau - s1
            count = jnp.zeros(s1.shape, F32)
            for row in r2:
                count = count + jnp.where(row >= c, 1.0, 0.0)
            n_scr[h] = count
            e1_scr[h] = jnp.exp(s1 - r1[0])
            e2 = (jnp.exp(s2 - r2[0]) / z).astype(BF16)
            rank2 = rank2.astype(BF16)
            for lg in range(n_lg):
                ls = slice(lg * LANES, (lg + 1) * LANES)
                r2_scr[h, lg] = rank2[:, ls].reshape(PACKED_TILES, PACKED_ROWS, LANES)
                e2_scr[h, lg] = e2[:, ls].reshape(PACKED_TILES, PACKED_ROWS, LANES)
            return carry

        lax.fori_loop(0, PEER_HEADS, head, 0)

        acc_scr[...] = jnp.zeros_like(acc_scr)
        _activations(u_ref[...], h2_scr, s_a, slice(None), n_lg)

    def step(s_w, s_r):
        a0 = pl.multiple_of((e - 1) * na, na)
        for h in range(PEER_HEADS):
            nslab = n_scr[h, pl.ds(a0, na), :]
            eslab = e1_scr[h, pl.ds(a0, na), :]
            for r in range(na):
                nb = jnp.broadcast_to(nslab[r:r + 1, :], (PACKED_ROWS, tt)).astype(BF16)
                eb = jnp.broadcast_to(eslab[r:r + 1, :], (PACKED_ROWS, tt)).astype(BF16)
                for lg in range(n_lg):
                    nb_scr[h, r, lg] = nb[:, lg * LANES:(lg + 1) * LANES]
                    eb_scr[h, r, lg] = eb[:, lg * LANES:(lg + 1) * LANES]

        zero = jnp.zeros((), BF16)
        halves = [tuple(range(lo, min(lo + 2, n_lg))) for lo in range(0, n_lg, 2)]
        blocks = [(half, ai, lg) for half in halves for ai in range(na) for lg in half]
        per_piece = len(blocks) // ACT_PIECES
        rows_per = na * PEER_NKEYS // ACT_PIECES
        ga = None
        for bi, (half, ai, lg) in enumerate(blocks):
            if bi % per_piece == 0:
                p = bi // per_piece
                start = p * rows_per
                if ga is not None:
                    probe = jnp.max(ga[0:PACKED_ROWS, :].astype(F32))
                    start = start - jnp.where(probe != probe, start, 0)
                start = pl.multiple_of(start, PACKED_ROWS)
                _activations(u_ref[pl.ds(start, rows_per), :], h2_scr, s_w,
                             pl.ds(start, rows_per), n_lg)
            gate = jnp.zeros((PACKED_TILES, PACKED_ROWS, LANES), BF16)
            for h in range(PEER_HEADS):
                hit = r2_scr[h, lg] < nb_scr[h, ai, lg][None]
                gate = gate + jnp.where(hit, e2_scr[h, lg], zero) * eb_scr[h, ai, lg][None]
            act = _gelu_tanh(s_r[lg, ai * PEER_NKEYS:(ai + 1) * PEER_NKEYS, :])
            ga = gate.reshape(PEER_NKEYS, LANES) * act.astype(BF16)
            ga_scr[ai * PEER_NKEYS:(ai + 1) * PEER_NKEYS, lg * LANES:(lg + 1) * LANES] = ga
            if ai == na - 1 and lg == half[-1]:
                hs = slice(half[0] * LANES, (half[-1] + 1) * LANES)
                acc_scr[:, hs] += jnp.dot(vt_ref[...], ga_scr[:, hs], preferred_element_type=F32)

    @pl.when((e > 0) & (e % 2 == 1))
    def _odd():
        step(s_b, s_a)

    @pl.when((e > 0) & (e % 2 == 0))
    def _even():
        step(s_a, s_b)

    @pl.when(e == n_chunks)
    def _fin():
        y = acc_scr[...].T
        o_ref[...] = _ln(ALPHA * x1_ref[...] + g2_ref[0] * y) * lng_ref[...] + lnb_ref[...]


def _peer(x1_flat, sc, sh, g2, wts, layer, tt, per_token, tiles_per_seq):
    t = x1_flat.shape[0]
    wqt_b, keys_b, u_b, vt_b, ln2_g, ln2_b = wts
    na = 8
    ec = na * PEER_NKEYS
    n_chunks = PEER_N // ec
    n_lg = tt // LANES
    mod = _mod_spec(per_token, tt, tiles_per_seq)
    return pl.pallas_call(
        functools.partial(_peer_kernel, tt=tt, na=na, n_chunks=n_chunks),
        grid=(t // tt, n_chunks + 1),
        in_specs=[
            pl.BlockSpec((tt, D_MODEL), lambda i, e: (i, 0)),
            mod, mod, mod,
            pl.BlockSpec((None, PEER_HEADS * PEER_DQ, D_MODEL), lambda i, e: (layer, 0, 0)),
            pl.BlockSpec((None, 2, PEER_NKEYS, PEER_DHALF), lambda i, e: (layer, 0, 0, 0)),
            pl.BlockSpec((None, ec, D_MODEL),
                         lambda i, e: (layer, jnp.minimum(e, n_chunks - 1), 0)),
            pl.BlockSpec((None, D_MODEL, ec), lambda i, e: (layer, 0, jnp.maximum(e - 1, 0))),
            pl.BlockSpec((None, 1, D_MODEL), lambda i, e: (layer, 0, 0)),
            pl.BlockSpec((None, 1, D_MODEL), lambda i, e: (layer, 0, 0)),
        ],
        out_specs=pl.BlockSpec((tt, D_MODEL), lambda i, e: (i, 0)),
        out_shape=jax.ShapeDtypeStruct((t, D_MODEL), F32),
        scratch_shapes=[
            pltpu.VMEM((tt, D_MODEL), BF16),
            pltpu.VMEM((PEER_HEADS * PEER_DQ, tt), BF16),
            pltpu.VMEM((PEER_HEADS, PEER_NKEYS, tt), F32),
            pltpu.VMEM((PEER_HEADS, PEER_NKEYS, tt), F32),
            pltpu.VMEM((PEER_HEADS, n_lg, PACKED_TILES, PACKED_ROWS, LANES), BF16),
            pltpu.VMEM((PEER_HEADS, n_lg, PACKED_TILES, PACKED_ROWS, LANES), BF16),
            pltpu.VMEM((PEER_HEADS, na, n_lg, PACKED_ROWS, LANES), BF16),
            pltpu.VMEM((PEER_HEADS, na, n_lg, PACKED_ROWS, LANES), BF16),
            pltpu.VMEM((n_lg, ec, LANES), F32),
            pltpu.VMEM((n_lg, ec, LANES), F32),
            pltpu.VMEM((ec, tt), BF16),
            pltpu.VMEM((D_MODEL, tt), F32),
        ],
        compiler_params=_params("arbitrary", "arbitrary"),
        name="peer",
    )(x1_flat, sc, sh, g2, wqt_b, keys_b, u_b, vt_b, ln2_g, ln2_b)


def _rope_tables(pos0, seq):
    half = DK_RET // 2
    inv = ROPE_BASE ** (-jnp.arange(half, dtype=F32) / half)
    ang = (pos0 + jnp.arange(seq)).astype(F32)[:, None] * inv[None, :]
    cos = jnp.cos(ang)
    sin = jnp.sin(ang)
    return jnp.concatenate([cos, cos], axis=-1), jnp.concatenate([-sin, sin], axis=-1)


def _trunk(x, mods, pool_hist, ret_state, pos0, wts, tiles):
    bsz, seq, _ = x.shape
    tt_proj, tt_mix, tt_peer = tiles
    per_token = seq < tt_peer
    cos2, sin2 = _rope_tables(pos0, seq)
    hist16 = jnp.pad(pool_hist, ((0, 0), (0, 0), (HIST_ROWS - POOL_HIST, 0), (0, 0)))
    new_hist, new_state = [], []
    for l in range(DEPTH):
        chunks = [mods[l, :, k * D_MODEL:(k + 1) * D_MODEL] for k in range(6)]
        seq_rows = [c[:, None, :] for c in chunks]
        if per_token:
            tok_rows = [jnp.repeat(c, seq, axis=0)[None] for c in chunks]
        else:
            tok_rows = seq_rows
        sh1, sc1, _, sh2, sc2, g2 = tok_rows
        g1 = seq_rows[2]
        proj = _proj(x.reshape(bsz * seq, D_MODEL), sc1, sh1, wts["w_in"], l,
                     tt_proj, per_token, seq // tt_proj if not per_token else 1)
        proj = proj.reshape(bsz, seq, D_IN)
        x1, s_l = _mix(proj, x, g1, cos2, sin2, hist16[l], ret_state[l],
                       (wts["w_grp"], wts["pool_scale"], wts["w_bp"], wts["w_br"], wts["w_out"],
                        wts["ln1_g"], wts["ln1_b"]), l, tt_mix, pos0)
        x2 = _peer(x1.reshape(bsz * seq, D_MODEL), sc2, sh2, g2,
                   (wts["w_qt"], wts["keys"], wts["u"], wts["vt"], wts["ln2_g"], wts["ln2_b"]),
                   l, tt_peer, per_token, seq // tt_peer if not per_token else 1)
        x = x2.reshape(bsz, seq, D_MODEL)
        new_hist.append(proj[:, seq - POOL_HIST:, :D_POOL])
        new_state.append(s_l)
    return x, jnp.stack(new_hist), jnp.stack(new_state)


def kernel(x_prompt, x_sample, cache_pool, state_ret, c_prompt, c_sample, w_ada, b_ada, w_in,
           w_pool_grp, pool_scale, w_branch_pool, w_branch_ret, w_out, ln1_g, ln1_b, w_peer_q,
           peer_sub_keys, peer_u, peer_v, ln2_g, ln2_b):
    n_prompt = x_prompt.shape[0]
    n_sample = x_sample.shape[0]
    wts = {
        "w_in": w_in.astype(BF16),
        "w_grp": w_pool_grp.astype(BF16),
        "pool_scale": pool_scale[:, None, :],
        "w_bp": w_branch_pool.astype(BF16),
        "w_br": w_branch_ret.astype(BF16),
        "w_out": w_out.astype(BF16),
        "ln1_g": ln1_g[:, None, :],
        "ln1_b": ln1_b[:, None, :],
        "w_qt": jnp.swapaxes(w_peer_q, 1, 2).astype(BF16),
        "keys": peer_sub_keys.astype(BF16),
        "u": peer_u.astype(BF16),
        "vt": jnp.swapaxes(peer_v, 1, 2).astype(BF16),
        "ln2_g": ln2_g[:, None, :],
        "ln2_b": ln2_b[:, None, :],
    }
    n_seq = n_prompt + n_sample
    c_pad = jnp.pad(jnp.concatenate([c_prompt, c_sample], axis=0), ((0, 16 - n_seq), (0, 0)))
    mods = _ada(c_pad, w_ada, b_ada)
    zero_hist = jnp.zeros((DEPTH, n_prompt, POOL_HIST, D_POOL), x_prompt.dtype)
    zero_state = jnp.zeros((DEPTH, n_prompt, N_RET_HEADS, DK_RET, DK_RET), F32)
    y_p, pool_p, ret_p = _trunk(x_prompt, mods[:, :n_prompt], zero_hist, zero_state, 0, wts,
                                (1024, 256, 512))
    y_s, pool_s, ret_s = _trunk(x_sample, mods[:, n_prompt:n_seq], cache_pool, state_ret,
                                PAST_LEN, wts, (256, 32, 256))
    return (y_p, y_s, pool_p, ret_p.astype(state_ret.dtype), pool_s, ret_s.astype(state_ret.dtype))
```

```python
import functools
import math

import jax
import jax.numpy as jnp
from jax import lax
from jax.experimental import pallas as pl
from jax.experimental.pallas import tpu as pltpu

F32 = jnp.float32
BF16 = jnp.bfloat16

D_MODEL = 1024
DEPTH = 2
PAST_LEN = 2048
POOL_WINDOWS = (2, 4, 8, 16)
D_POOL = 512
POOL_GROUP = 128
POOL_HIST = 15
HIST_ROWS = 16
N_RET_HEADS = 8
DK_RET = 128
D_RET = N_RET_HEADS * DK_RET
RET_SCALE = DK_RET ** -0.5
ROPE_BASE = 10000.0
PEER_HEADS = 8
PEER_NKEYS = 128
PEER_N = PEER_NKEYS * PEER_NKEYS
PEER_DQ = 256
PEER_DHALF = 128
PEER_TOPK = 16
LN_EPS = 1e-5
ALPHA = (2 * DEPTH) ** 0.25
OFF_Q = D_POOL
OFF_K = OFF_Q + D_RET
OFF_V = OFF_K + D_RET
OFF_G = OFF_V + D_RET
OFF_BG = OFF_G + D_RET
D_IN = OFF_BG + 2 * D_MODEL
LOG_G = tuple(math.log(1.0 - 2.0 ** (-5.0 - h)) for h in range(N_RET_HEADS))
LANES = 128
SUBLANES = 8
PACKED_ROWS = 16
PACKED_TILES = PEER_NKEYS // PACKED_ROWS
ACT_PIECES = 4
VMEM_LIMIT = 56 * 1024 * 1024
NEG_INF = float("-inf")


def _ln(x):
    mu = jnp.mean(x, axis=-1, keepdims=True)
    xc = x - mu
    var = jnp.mean(xc * xc, axis=-1, keepdims=True)
    return xc * lax.rsqrt(var + LN_EPS)


def _gelu_tanh(x):
    inner = math.sqrt(2.0 / math.pi) * (x + 0.044715 * (x * x * x))
    return 0.5 * x * (1.0 + jnp.tanh(inner))


def _params(*sem):
    return pltpu.CompilerParams(dimension_semantics=sem, vmem_limit_bytes=VMEM_LIMIT)


def _ada_kernel(c_ref, w_ref, b_ref, o_ref):
    c = c_ref[...]
    a = (c * jax.nn.sigmoid(c)).astype(BF16)
    o_ref[...] = jnp.dot(a, w_ref[...].astype(BF16), preferred_element_type=F32) + b_ref[...]


def _ada(c_pad, w_ada, b_ada):
    rows = c_pad.shape[0]
    n_out = w_ada.shape[-1]
    ct = 1536
    return pl.pallas_call(
        _ada_kernel,
        grid=(DEPTH, n_out // ct),
        in_specs=[
            pl.BlockSpec((rows, D_MODEL), lambda l, j: (0, 0)),
            pl.BlockSpec((None, D_MODEL, ct), lambda l, j: (l, 0, j)),
            pl.BlockSpec((None, 1, ct), lambda l, j: (l, 0, j)),
        ],
        out_specs=pl.BlockSpec((None, rows, ct), lambda l, j: (l, 0, j)),
        out_shape=jax.ShapeDtypeStruct((DEPTH, rows, n_out), F32),
        compiler_params=_params("arbitrary", "arbitrary"),
        name="ada",
    )(c_pad, w_ada, b_ada.reshape(DEPTH, 1, n_out))


def _mod_spec(per_token, tt, tiles_per_seq):
    if per_token:
        return pl.BlockSpec((1, tt, D_MODEL), lambda i, *_: (0, i, 0))
    return pl.BlockSpec((1, 1, D_MODEL), lambda i, *_: (i // tiles_per_seq, 0, 0))


def _proj_kernel(x_ref, sc_ref, sh_ref, w_ref, o_ref, h_scr):
    @pl.when(pl.program_id(1) == 0)
    def _():
        h = _ln(x_ref[...]) * (1.0 + sc_ref[0]) + sh_ref[0]
        h_scr[...] = h.astype(BF16)

    o_ref[...] = jnp.dot(h_scr[...], w_ref[...], preferred_element_type=F32)


def _proj(x_flat, sc, sh, w_in_b, layer, tt, per_token, tiles_per_seq):
    t = x_flat.shape[0]
    nt = D_IN // 4
    return pl.pallas_call(
        _proj_kernel,
        grid=(t // tt, D_IN // nt),
        in_specs=[
            pl.BlockSpec((tt, D_MODEL), lambda i, j: (i, 0)),
            _mod_spec(per_token, tt, tiles_per_seq),
            _mod_spec(per_token, tt, tiles_per_seq),
            pl.BlockSpec((None, D_MODEL, nt), lambda i, j: (layer, 0, j)),
        ],
        out_specs=pl.BlockSpec((tt, nt), lambda i, j: (i, j)),
        out_shape=jax.ShapeDtypeStruct((t, D_IN), F32),
        scratch_shapes=[pltpu.VMEM((tt, D_MODEL), BF16)],
        compiler_params=_params("arbitrary", "arbitrary"),
        name="proj",
    )(x_flat, sc, sh, w_in_b)


def _mix_kernel(proj_ref, x_ref, g1_ref, cos_ref, sin_ref, hist_ref, s0_ref,
                wgrp_ref, pscale_ref, wbp_ref, wbr_ref, wout_ref, lng_ref, lnb_ref,
                x1_ref, sfin_ref,
                state_scr, ext_scr, decay_scr, xi_scr, zeta_scr, retg_scr, pool_scr,
                *, tt, pos0, n_tiles):
    j = pl.program_id(1)

    @pl.when(j == 0)
    def _init():
        state_scr[...] = s0_ref[0]
        ext_scr[0:HIST_ROWS, :] = hist_ref[0]
        row = lax.broadcasted_iota(jnp.int32, (tt, tt), 0)
        col = lax.broadcasted_iota(jnp.int32, (tt, tt), 1)
        causal = row >= col
        diff = jnp.where(causal, (row - col).astype(F32), 0.0)
        rowl = lax.broadcasted_iota(jnp.int32, (tt, LANES), 0).astype(F32)
        for h in range(N_RET_HEADS):
            decay_scr[h] = jnp.where(causal, jnp.exp(LOG_G[h] * diff), 0.0)
            xi_scr[h] = jnp.exp(LOG_G[h] * (rowl + 1.0))
            zeta_scr[h] = jnp.exp(LOG_G[h] * (tt - 1.0 - rowl))

    p = proj_ref[0, :, 0:D_POOL]
    ext_scr[HIST_ROWS:HIST_ROWS + tt, :] = p
    pos = pos0 + j * tt + lax.broadcasted_iota(jnp.int32, (tt, POOL_GROUP), 0)
    for g, w in enumerate(POOL_WINDOWS):
        cs = slice(g * POOL_GROUP, (g + 1) * POOL_GROUP)
        ws = ext_scr[HIST_ROWS:HIST_ROWS + tt, cs]
        for d in range(1, w):
            ws = ws + ext_scr[HIST_ROWS - d:HIST_ROWS - d + tt, cs]
        cnt = jnp.minimum(pos + 1, w).astype(F32)
        pooled = ws / cnt - p[:, cs]
        mixed = jnp.dot(pooled.astype(BF16), wgrp_ref[g], preferred_element_type=F32)
        pool_scr[:, cs] = (mixed * pscale_ref[:, cs]).astype(BF16)
    ext_scr[0:HIST_ROWS, :] = ext_scr[tt:tt + HIST_ROWS, :]

    cosv = cos_ref[...]
    sinv = sin_ref[...]
    for h in range(N_RET_HEADS):
        hs = slice(h * DK_RET, (h + 1) * DK_RET)
        q = proj_ref[0, :, OFF_Q + h * DK_RET:OFF_Q + (h + 1) * DK_RET]
        k = proj_ref[0, :, OFF_K + h * DK_RET:OFF_K + (h + 1) * DK_RET]
        v = proj_ref[0, :, OFF_V + h * DK_RET:OFF_V + (h + 1) * DK_RET].astype(BF16)
        g = proj_ref[0, :, OFF_G + h * DK_RET:OFF_G + (h + 1) * DK_RET]
        qr = q * cosv + pltpu.roll(q, DK_RET // 2, 1) * sinv
        kr = (k * cosv + pltpu.roll(k, DK_RET // 2, 1) * sinv) * RET_SCALE
        qb = qr.astype(BF16)
        scores = lax.dot_general(qb, kr.astype(BF16), (((1,), (1,)), ((), ())),
                                 preferred_element_type=F32) * decay_scr[h]
        intra = jnp.dot(scores.astype(BF16), v, preferred_element_type=F32)
        s_prev = state_scr[h]
        cross = jnp.dot(qb, s_prev.astype(BF16), preferred_element_type=F32) * xi_scr[h]
        kz = (kr * zeta_scr[h]).astype(BF16)
        kv = lax.dot_general(kz, v, (((0,), (0,)), ((), ())), preferred_element_type=F32)
        state_scr[h] = math.exp(LOG_G[h] * tt) * s_prev + kv
        retg_scr[:, hs] = (_ln(intra + cross) * (g * jax.nn.sigmoid(g))).astype(BF16)

    bgp = jax.nn.sigmoid(proj_ref[0, :, OFF_BG:OFF_BG + D_MODEL])
    bgr = jax.nn.sigmoid(proj_ref[0, :, OFF_BG + D_MODEL:OFF_BG + 2 * D_MODEL])
    pb = jnp.dot(pool_scr[...], wbp_ref[...], preferred_element_type=F32)
    rb = jnp.dot(retg_scr[...], wbr_ref[...], preferred_element_type=F32)
    merged = (bgp * pb + bgr * rb).astype(BF16)
    z = jnp.dot(merged, wout_ref[...], preferred_element_type=F32)
    x1_ref[0] = _ln(ALPHA * x_ref[0] + g1_ref[0] * z) * lng_ref[...] + lnb_ref[...]

    @pl.when(j == n_tiles - 1)
    def _fin():
        sfin_ref[0] = state_scr[...]


def _mix(proj, x, g1, cos2, sin2, hist16, s0, wts, layer, tt, pos0):
    bsz, seq, _ = x.shape
    n_tiles = seq // tt
    wgrp_b, pscale, wbp_b, wbr_b, wout_b, ln1_g, ln1_b = wts
    const2 = lambda b, j: (layer, 0, 0)
    return pl.pallas_call(
        functools.partial(_mix_kernel, tt=tt, pos0=pos0, n_tiles=n_tiles),
        grid=(bsz, n_tiles),
        in_specs=[
            pl.BlockSpec((1, tt, D_IN), lambda b, j: (b, j, 0)),
            pl.BlockSpec((1, tt, D_MODEL), lambda b, j: (b, j, 0)),
            pl.BlockSpec((1, 1, D_MODEL), lambda b, j: (b, 0, 0)),
            pl.BlockSpec((tt, LANES), lambda b, j: (j, 0)),
            pl.BlockSpec((tt, LANES), lambda b, j: (j, 0)),
            pl.BlockSpec((1, HIST_ROWS, D_POOL), lambda b, j: (b, 0, 0)),
            pl.BlockSpec((1, N_RET_HEADS, DK_RET, DK_RET), lambda b, j: (b, 0, 0, 0)),
            pl.BlockSpec((None, len(POOL_WINDOWS), POOL_GROUP, POOL_GROUP),
                         lambda b, j: (layer, 0, 0, 0)),
            pl.BlockSpec((None, 1, D_POOL), const2),
            pl.BlockSpec((None, D_POOL, D_MODEL), const2),
            pl.BlockSpec((None, D_RET, D_MODEL), const2),
            pl.BlockSpec((None, D_MODEL, D_MODEL), const2),
            pl.BlockSpec((None, 1, D_MODEL), const2),
            pl.BlockSpec((None, 1, D_MODEL), const2),
        ],
        out_specs=[
            pl.BlockSpec((1, tt, D_MODEL), lambda b, j: (b, j, 0)),
            pl.BlockSpec((1, N_RET_HEADS, DK_RET, DK_RET), lambda b, j: (b, 0, 0, 0)),
        ],
        out_shape=[
            jax.ShapeDtypeStruct((bsz, seq, D_MODEL), F32),
            jax.ShapeDtypeStruct((bsz, N_RET_HEADS, DK_RET, DK_RET), F32),
        ],
        scratch_shapes=[
            pltpu.VMEM((N_RET_HEADS, DK_RET, DK_RET), F32),
            pltpu.VMEM((HIST_ROWS + tt, D_POOL), F32),
            pltpu.VMEM((N_RET_HEADS, tt, tt), F32),
            pltpu.VMEM((N_RET_HEADS, tt, LANES), F32),
            pltpu.VMEM((N_RET_HEADS, tt, LANES), F32),
            pltpu.VMEM((tt, D_RET), BF16),
            pltpu.VMEM((tt, D_POOL), BF16),
        ],
        compiler_params=_params("arbitrary", "arbitrary"),
        name="mix",
    )(proj, x, g1, cos2, sin2, hist16, s0, wgrp_b, pscale, wbp_b, wbr_b, wout_b, ln1_g, ln1_b)


def _top16_rows(s, with_rank=False):
    rows = []
    cur = s
    rank = jnp.full(s.shape, float(PEER_TOPK), F32) if with_rank else None
    for i in range(PEER_TOPK):
        m = jnp.max(cur, axis=0, keepdims=True)
        rows.append(m)
        hit = cur == m
        if with_rank:
            rank = jnp.where(hit, float(i), rank)
        if i + 1 < PEER_TOPK:
            cur = jnp.where(hit, NEG_INF, cur)
    return (rows, rank) if with_rank else rows


def _stack_rows(rows, tt):
    ridx = lax.broadcasted_iota(jnp.int32, (PEER_TOPK, tt), 0)
    out = jnp.zeros((PEER_TOPK, tt), F32)
    for i, r in enumerate(rows):
        out = jnp.where(ridx == i, r, out)
    return out


def _activations(u_rows, h2_scr, s_dst, rows, n_lg):
    s_val = lax.dot_general(u_rows, h2_scr[...], (((1,), (1,)), ((), ())),
                            preferred_element_type=F32)
    for lg in range(n_lg):
        s_dst[lg, rows, :] = s_val[:, lg * LANES:(lg + 1) * LANES]


def _peer_kernel(x1_ref, sc_ref, sh_ref, g2_ref, wqt_ref, keys_ref, u_ref, vt_ref,
                 lng_ref, lnb_ref, o_ref,
                 h2_scr, qt_scr, n_scr, e1_scr, r2_scr, e2_scr, nb_scr, eb_scr, s_scr, ga_scr, acc_scr,
                 *, tt, na, n_chunks):
    e = pl.program_id(1)
    n_lg = tt // LANES

    @pl.when(e == 0)
    def _prep():
        h2 = (_ln(x1_ref[...]) * (1.0 + sc_ref[0]) + sh_ref[0]).astype(BF16)
        h2_scr[...] = h2
        qt = lax.dot_general(wqt_ref[...], h2, (((1,), (1,)), ((), ())),
                             preferred_element_type=F32)
        qt_scr[...] = qt.astype(BF16)

        def head(h, carry):
            base = pl.multiple_of(h * PEER_DQ, PEER_DQ)
            s1 = jnp.dot(keys_ref[0], qt_scr[pl.ds(base, PEER_DHALF), :],
                         preferred_element_type=F32)
            s2 = jnp.dot(keys_ref[1], qt_scr[pl.ds(base + PEER_DHALF, PEER_DHALF), :],
                         preferred_element_type=F32)
            r1 = _top16_rows(s1)
            r2, rank2 = _top16_rows(s2, with_rank=True)
            v1 = _stack_rows(r1, tt)
            v2 = _stack_rows(r2, tt)
            ridx = lax.broadcasted_iota(jnp.int32, (PEER_TOPK, tt), 0)
            half = PEER_TOPK // 2
            cand = [jnp.where(ridx < PEER_TOPK // (i + 1), r1[i] + v2, NEG_INF)
                    for i in range(half)]
            cand.append(jnp.where(ridx >= half, v1 + r2[0], NEG_INF))
            cur = cand
            tau = None
            for it in range(PEER_TOPK):
                mx = functools.reduce(jnp.maximum, cur)
                tau = jnp.max(mx, axis=0, keepdims=True)
                if it + 1 < PEER_TOPK:
                    cur = [jnp.where(c == tau, NEG_INF, c) for c in cur]
            top = r1[0] + r2[0]
            zsum = jnp.zeros((PEER_TOPK, tt), F32)
            for c in cand:
                zsum = zsum + jnp.where(c >= tau, jnp.exp(c - top), 0.0)
            z = jnp.sum(zsum, axis=0, keepdims=True)
            c = tau - s1
            count = jnp.zeros(s1.shape, F32)
            for row in r2:
                count = count + jnp.where(row >= c, 1.0, 0.0)
            n_scr[h] = count
            e1_scr[h] = jnp.exp(s1 - r1[0])
            e2 = (jnp.exp(s2 - r2[0]) / z).astype(BF16)
            rank2 = rank2.astype(BF16)
            for lg in range(n_lg):
                ls = slice(lg * LANES, (lg + 1) * LANES)
                r2_scr[h, lg] = rank2[:, ls].reshape(PACKED_TILES, PACKED_ROWS, LANES)
                e2_scr[h, lg] = e2[:, ls].reshape(PACKED_TILES, PACKED_ROWS, LANES)
            return carry

        lax.fori_loop(0, PEER_HEADS, head, 0)

    _activations(u_ref[...], h2_scr, s_scr, slice(None), n_lg)

    a0 = pl.multiple_of(e * na, na)
    for h in range(PEER_HEADS):
        nslab = n_scr[h, pl.ds(a0, na), :]
        eslab = e1_scr[h, pl.ds(a0, na), :]
        for r in range(na):
            nb = jnp.broadcast_to(nslab[r:r + 1, :], (PACKED_ROWS, tt)).astype(BF16)
            eb = jnp.broadcast_to(eslab[r:r + 1, :], (PACKED_ROWS, tt)).astype(BF16)
            for lg in range(n_lg):
                nb_scr[h, r, lg] = nb[:, lg * LANES:(lg + 1) * LANES]
                eb_scr[h, r, lg] = eb[:, lg * LANES:(lg + 1) * LANES]

    zero = jnp.zeros((), BF16)
    for lg in range(n_lg):
        for ai in range(na):
            gate = jnp.zeros((PACKED_TILES, PACKED_ROWS, LANES), BF16)
            for h in range(PEER_HEADS):
                hit = r2_scr[h, lg] < nb_scr[h, ai, lg][None]
                gate = gate + jnp.where(hit, e2_scr[h, lg], zero) * eb_scr[h, ai, lg][None]
            act = _gelu_tanh(s_scr[lg, ai * PEER_NKEYS:(ai + 1) * PEER_NKEYS, :])
            ga = gate.reshape(PEER_NKEYS, LANES) * act.astype(BF16)
            ga_scr[ai * PEER_NKEYS:(ai + 1) * PEER_NKEYS, lg * LANES:(lg + 1) * LANES] = ga

    contrib = jnp.dot(vt_ref[...], ga_scr[...], preferred_element_type=F32)

    @pl.when(e == 0)
    def _():
        acc_scr[...] = contrib

    @pl.when(e > 0)
    def _():
        acc_scr[...] += contrib

    @pl.when(e == n_chunks - 1)
    def _fin():
        y = acc_scr[...].T
        o_ref[...] = _ln(ALPHA * x1_ref[...] + g2_ref[0] * y) * lng_ref[...] + lnb_ref[...]


def _peer(x1_flat, sc, sh, g2, wts, layer, tt, per_token, tiles_per_seq):
    t = x1_flat.shape[0]
    wqt_b, keys_b, u_b, vt_b, ln2_g, ln2_b = wts
    na = 8
    ec = na * PEER_NKEYS
    n_chunks = PEER_N // ec
    n_lg = tt // LANES
    mod = _mod_spec(per_token, tt, tiles_per_seq)
    return pl.pallas_call(
        functools.partial(_peer_kernel, tt=tt, na=na, n_chunks=n_chunks),
        grid=(t // tt, n_chunks),
        in_specs=[
            pl.BlockSpec((tt, D_MODEL), lambda i, e: (i, 0)),
            mod, mod, mod,
            pl.BlockSpec((None, PEER_HEADS * PEER_DQ, D_MODEL), lambda i, e: (layer, 0, 0)),
            pl.BlockSpec((None, 2, PEER_NKEYS, PEER_DHALF), lambda i, e: (layer, 0, 0, 0)),
            pl.BlockSpec((None, ec, D_MODEL), lambda i, e: (layer, e, 0)),
            pl.BlockSpec((None, D_MODEL, ec), lambda i, e: (layer, 0, e)),
            pl.BlockSpec((None, 1, D_MODEL), lambda i, e: (layer, 0, 0)),
            pl.BlockSpec((None, 1, D_MODEL), lambda i, e: (layer, 0, 0)),
        ],
        out_specs=pl.BlockSpec((tt, D_MODEL), lambda i, e: (i, 0)),
        out_shape=jax.ShapeDtypeStruct((t, D_MODEL), F32),
        scratch_shapes=[
            pltpu.VMEM((tt, D_MODEL), BF16),
            pltpu.VMEM((PEER_HEADS * PEER_DQ, tt), BF16),
            pltpu.VMEM((PEER_HEADS, PEER_NKEYS, tt), F32),
            pltpu.VMEM((PEER_HEADS, PEER_NKEYS, tt), F32),
            pltpu.VMEM((PEER_HEADS, n_lg, PACKED_TILES, PACKED_ROWS, LANES), BF16),
            pltpu.VMEM((PEER_HEADS, n_lg, PACKED_TILES, PACKED_ROWS, LANES), BF16),
            pltpu.VMEM((PEER_HEADS, na, n_lg, PACKED_ROWS, LANES), BF16),
            pltpu.VMEM((PEER_HEADS, na, n_lg, PACKED_ROWS, LANES), BF16),
            pltpu.VMEM((n_lg, ec, LANES), F32),
            pltpu.VMEM((ec, tt), BF16),
            pltpu.VMEM((D_MODEL, tt), F32),
        ],
        compiler_params=_params("arbitrary", "arbitrary"),
        name="peer",
    )(x1_flat, sc, sh, g2, wqt_b, keys_b, u_b, vt_b, ln2_g, ln2_b)


def _rope_tables(pos0, seq):
    half = DK_RET // 2
    inv = ROPE_BASE ** (-jnp.arange(half, dtype=F32) / half)
    ang = (pos0 + jnp.arange(seq)).astype(F32)[:, None] * inv[None, :]
    cos = jnp.cos(ang)
    sin = jnp.sin(ang)
    return jnp.concatenate([cos, cos], axis=-1), jnp.concatenate([-sin, sin], axis=-1)


def _trunk(x, mods, pool_hist, ret_state, pos0, wts, tiles):
    bsz, seq, _ = x.shape
    tt_proj, tt_mix, tt_peer = tiles
    per_token = seq < tt_peer
    cos2, sin2 = _rope_tables(pos0, seq)
    hist16 = jnp.pad(pool_hist, ((0, 0), (0, 0), (HIST_ROWS - POOL_HIST, 0), (0, 0)))
    new_hist, new_state = [], []
    for l in range(DEPTH):
        chunks = [mods[l, :, k * D_MODEL:(k + 1) * D_MODEL] for k in range(6)]
        seq_rows = [c[:, None, :] for c in chunks]
        if per_token:
            tok_rows = [jnp.repeat(c, seq, axis=0)[None] for c in chunks]
        else:
            tok_rows = seq_rows
        sh1, sc1, _, sh2, sc2, g2 = tok_rows
        g1 = seq_rows[2]
        proj = _proj(x.reshape(bsz * seq, D_MODEL), sc1, sh1, wts["w_in"], l,
                     tt_proj, per_token, seq // tt_proj if not per_token else 1)
        proj = proj.reshape(bsz, seq, D_IN)
        x1, s_l = _mix(proj, x, g1, cos2, sin2, hist16[l], ret_state[l],
                       (wts["w_grp"], wts["pool_scale"], wts["w_bp"], wts["w_br"], wts["w_out"],
                        wts["ln1_g"], wts["ln1_b"]), l, tt_mix, pos0)
        x2 = _peer(x1.reshape(bsz * seq, D_MODEL), sc2, sh2, g2,
                   (wts["w_qt"], wts["keys"], wts["u"], wts["vt"], wts["ln2_g"], wts["ln2_b"]),
                   l, tt_peer, per_token, seq // tt_peer if not per_token else 1)
        x = x2.reshape(bsz, seq, D_MODEL)
        new_hist.append(proj[:, seq - POOL_HIST:, :D_POOL])
        new_state.append(s_l)
    return x, jnp.stack(new_hist), jnp.stack(new_state)


def kernel(x_prompt, x_sample, cache_pool, state_ret, c_prompt, c_sample, w_ada, b_ada, w_in,
           w_pool_grp, pool_scale, w_branch_pool, w_branch_ret, w_out, ln1_g, ln1_b, w_peer_q,
           peer_sub_keys, peer_u, peer_v, ln2_g, ln2_b):
    n_prompt = x_prompt.shape[0]
    n_sample = x_sample.shape[0]
    wts = {
        "w_in": w_in.astype(BF16),
        "w_grp": w_pool_grp.astype(BF16),
        "pool_scale": pool_scale[:, None, :],
        "w_bp": w_branch_pool.astype(BF16),
        "w_br": w_branch_ret.astype(BF16),
        "w_out": w_out.astype(BF16),
        "ln1_g": ln1_g[:, None, :],
        "ln1_b": ln1_b[:, None, :],
        "w_qt": jnp.swapaxes(w_peer_q, 1, 2).astype(BF16),
        "keys": peer_sub_keys.astype(BF16),
        "u": peer_u.astype(BF16),
        "vt": jnp.swapaxes(peer_v, 1, 2).astype(BF16),
        "ln2_g": ln2_g[:, None, :],
        "ln2_b": ln2_b[:, None, :],
    }
    n_seq = n_prompt + n_sample
    c_pad = jnp.pad(jnp.concatenate([c_prompt, c_sample], axis=0), ((0, 16 - n_seq), (0, 0)))
    mods = _ada(c_pad, w_ada, b_ada)
    zero_hist = jnp.zeros((DEPTH, n_prompt, POOL_HIST, D_POOL), x_prompt.dtype)
    zero_state = jnp.zeros((DEPTH, n_prompt, N_RET_HEADS, DK_RET, DK_RET), F32)
    y_p, pool_p, ret_p = _trunk(x_prompt, mods[:, :n_prompt], zero_hist, zero_state, 0, wts,
                                (1024, 256, 512))
    y_s, pool_s, ret_s = _trunk(x_sample, mods[:, n_prompt:n_seq], cache_pool, state_ret,
                                PAST_LEN, wts, (256, 32, 256))
    return (y_p, y_s, pool_p, ret_p.astype(state_ret.dtype), pool_s, ret_s.astype(state_ret.dtype))
```

```python
import functools
import math

import jax
import jax.numpy as jnp
from jax import lax
from jax.experimental import pallas as pl
from jax.experimental.pallas import tpu as pltpu

F32 = jnp.float32
BF16 = jnp.bfloat16

D_MODEL = 1024
DEPTH = 2
PAST_LEN = 2048
POOL_WINDOWS = (2, 4, 8, 16)
D_POOL = 512
POOL_GROUP = 128
POOL_HIST = 15
HIST_ROWS = 16
N_RET_HEADS = 8
DK_RET = 128
D_RET = N_RET_HEADS * DK_RET
RET_SCALE = DK_RET ** -0.5
ROPE_BASE = 10000.0
PEER_HEADS = 8
PEER_NKEYS = 128
PEER_N = PEER_NKEYS * PEER_NKEYS
PEER_DQ = 256
PEER_DHALF = 128
PEER_TOPK = 16
LN_EPS = 1e-5
ALPHA = (2 * DEPTH) ** 0.25
OFF_Q = D_POOL
OFF_K = OFF_Q + D_RET
OFF_V = OFF_K + D_RET
OFF_G = OFF_V + D_RET
OFF_BG = OFF_G + D_RET
D_IN = OFF_BG + 2 * D_MODEL
LOG_G = tuple(math.log(1.0 - 2.0 ** (-5.0 - h)) for h in range(N_RET_HEADS))
LANES = 128
SUBLANES = 8
KEY_TILES = PEER_NKEYS // SUBLANES
PACKED_ROWS = 16
ACT_PIECES = 4
VMEM_LIMIT = 56 * 1024 * 1024
NEG_INF = float("-inf")


def _ln(x):
    mu = jnp.mean(x, axis=-1, keepdims=True)
    xc = x - mu
    var = jnp.mean(xc * xc, axis=-1, keepdims=True)
    return xc * lax.rsqrt(var + LN_EPS)


def _gelu_tanh(x):
    inner = math.sqrt(2.0 / math.pi) * (x + 0.044715 * (x * x * x))
    return 0.5 * x * (1.0 + jnp.tanh(inner))


def _params(*sem):
    return pltpu.CompilerParams(dimension_semantics=sem, vmem_limit_bytes=VMEM_LIMIT)


def _ada_kernel(c_ref, w_ref, b_ref, o_ref):
    c = c_ref[...]
    a = (c * jax.nn.sigmoid(c)).astype(BF16)
    o_ref[...] = jnp.dot(a, w_ref[...].astype(BF16), preferred_element_type=F32) + b_ref[...]


def _ada(c_pad, w_ada, b_ada):
    rows = c_pad.shape[0]
    n_out = w_ada.shape[-1]
    ct = 1536
    return pl.pallas_call(
        _ada_kernel,
        grid=(DEPTH, n_out // ct),
        in_specs=[
            pl.BlockSpec((rows, D_MODEL), lambda l, j: (0, 0)),
            pl.BlockSpec((None, D_MODEL, ct), lambda l, j: (l, 0, j)),
            pl.BlockSpec((None, 1, ct), lambda l, j: (l, 0, j)),
        ],
        out_specs=pl.BlockSpec((None, rows, ct), lambda l, j: (l, 0, j)),
        out_shape=jax.ShapeDtypeStruct((DEPTH, rows, n_out), F32),
        compiler_params=_params("arbitrary", "arbitrary"),
        name="ada",
    )(c_pad, w_ada, b_ada.reshape(DEPTH, 1, n_out))


def _mod_spec(per_token, tt, tiles_per_seq):
    if per_token:
        return pl.BlockSpec((1, tt, D_MODEL), lambda i, *_: (0, i, 0))
    return pl.BlockSpec((1, 1, D_MODEL), lambda i, *_: (i // tiles_per_seq, 0, 0))


def _proj_kernel(x_ref, sc_ref, sh_ref, w_ref, o_ref, h_scr):
    @pl.when(pl.program_id(1) == 0)
    def _():
        h = _ln(x_ref[...]) * (1.0 + sc_ref[0]) + sh_ref[0]
        h_scr[...] = h.astype(BF16)

    o_ref[...] = jnp.dot(h_scr[...], w_ref[...], preferred_element_type=F32)


def _proj(x_flat, sc, sh, w_in_b, layer, tt, per_token, tiles_per_seq):
    t = x_flat.shape[0]
    nt = D_IN // 4
    return pl.pallas_call(
        _proj_kernel,
        grid=(t // tt, D_IN // nt),
        in_specs=[
            pl.BlockSpec((tt, D_MODEL), lambda i, j: (i, 0)),
            _mod_spec(per_token, tt, tiles_per_seq),
            _mod_spec(per_token, tt, tiles_per_seq),
            pl.BlockSpec((None, D_MODEL, nt), lambda i, j: (layer, 0, j)),
        ],
        out_specs=pl.BlockSpec((tt, nt), lambda i, j: (i, j)),
        out_shape=jax.ShapeDtypeStruct((t, D_IN), F32),
        scratch_shapes=[pltpu.VMEM((tt, D_MODEL), BF16)],
        compiler_params=_params("arbitrary", "arbitrary"),
        name="proj",
    )(x_flat, sc, sh, w_in_b)


def _mix_kernel(proj_ref, x_ref, g1_ref, cos_ref, sin_ref, hist_ref, s0_ref,
                wgrp_ref, pscale_ref, wbp_ref, wbr_ref, wout_ref, lng_ref, lnb_ref,
                x1_ref, sfin_ref,
                state_scr, ext_scr, decay_scr, xi_scr, zeta_scr, retg_scr, pool_scr,
                *, tt, pos0, n_tiles):
    j = pl.program_id(1)

    @pl.when(j == 0)
    def _init():
        state_scr[...] = s0_ref[0]
        ext_scr[0:HIST_ROWS, :] = hist_ref[0]
        row = lax.broadcasted_iota(jnp.int32, (tt, tt), 0)
        col = lax.broadcasted_iota(jnp.int32, (tt, tt), 1)
        causal = row >= col
        diff = jnp.where(causal, (row - col).astype(F32), 0.0)
        rowl = lax.broadcasted_iota(jnp.int32, (tt, LANES), 0).astype(F32)
        for h in range(N_RET_HEADS):
            decay_scr[h] = jnp.where(causal, jnp.exp(LOG_G[h] * diff), 0.0)
            xi_scr[h] = jnp.exp(LOG_G[h] * (rowl + 1.0))
            zeta_scr[h] = jnp.exp(LOG_G[h] * (tt - 1.0 - rowl))

    p = proj_ref[0, :, 0:D_POOL]
    ext_scr[HIST_ROWS:HIST_ROWS + tt, :] = p
    pos = pos0 + j * tt + lax.broadcasted_iota(jnp.int32, (tt, POOL_GROUP), 0)
    for g, w in enumerate(POOL_WINDOWS):
        cs = slice(g * POOL_GROUP, (g + 1) * POOL_GROUP)
        ws = ext_scr[HIST_ROWS:HIST_ROWS + tt, cs]
        for d in range(1, w):
            ws = ws + ext_scr[HIST_ROWS - d:HIST_ROWS - d + tt, cs]
        cnt = jnp.minimum(pos + 1, w).astype(F32)
        pooled = ws / cnt - p[:, cs]
        mixed = jnp.dot(pooled.astype(BF16), wgrp_ref[g], preferred_element_type=F32)
        pool_scr[:, cs] = (mixed * pscale_ref[:, cs]).astype(BF16)
    ext_scr[0:HIST_ROWS, :] = ext_scr[tt:tt + HIST_ROWS, :]

    cosv = cos_ref[...]
    sinv = sin_ref[...]
    for h in range(N_RET_HEADS):
        hs = slice(h * DK_RET, (h + 1) * DK_RET)
        q = proj_ref[0, :, OFF_Q + h * DK_RET:OFF_Q + (h + 1) * DK_RET]
        k = proj_ref[0, :, OFF_K + h * DK_RET:OFF_K + (h + 1) * DK_RET]
        v = proj_ref[0, :, OFF_V + h * DK_RET:OFF_V + (h + 1) * DK_RET].astype(BF16)
        g = proj_ref[0, :, OFF_G + h * DK_RET:OFF_G + (h + 1) * DK_RET]
        qr = q * cosv + pltpu.roll(q, DK_RET // 2, 1) * sinv
        kr = (k * cosv + pltpu.roll(k, DK_RET // 2, 1) * sinv) * RET_SCALE
        qb = qr.astype(BF16)
        scores = lax.dot_general(qb, kr.astype(BF16), (((1,), (1,)), ((), ())),
                                 preferred_element_type=F32) * decay_scr[h]
        intra = jnp.dot(scores.astype(BF16), v, preferred_element_type=F32)
        s_prev = state_scr[h]
        cross = jnp.dot(qb, s_prev.astype(BF16), preferred_element_type=F32) * xi_scr[h]
        kz = (kr * zeta_scr[h]).astype(BF16)
        kv = lax.dot_general(kz, v, (((0,), (0,)), ((), ())), preferred_element_type=F32)
        state_scr[h] = math.exp(LOG_G[h] * tt) * s_prev + kv
        retg_scr[:, hs] = (_ln(intra + cross) * (g * jax.nn.sigmoid(g))).astype(BF16)

    bgp = jax.nn.sigmoid(proj_ref[0, :, OFF_BG:OFF_BG + D_MODEL])
    bgr = jax.nn.sigmoid(proj_ref[0, :, OFF_BG + D_MODEL:OFF_BG + 2 * D_MODEL])
    pb = jnp.dot(pool_scr[...], wbp_ref[...], preferred_element_type=F32)
    rb = jnp.dot(retg_scr[...], wbr_ref[...], preferred_element_type=F32)
    merged = (bgp * pb + bgr * rb).astype(BF16)
    z = jnp.dot(merged, wout_ref[...], preferred_element_type=F32)
    x1_ref[0] = _ln(ALPHA * x_ref[0] + g1_ref[0] * z) * lng_ref[...] + lnb_ref[...]

    @pl.when(j == n_tiles - 1)
    def _fin():
        sfin_ref[0] = state_scr[...]


def _mix(proj, x, g1, cos2, sin2, hist16, s0, wts, layer, tt, pos0):
    bsz, seq, _ = x.shape
    n_tiles = seq // tt
    wgrp_b, pscale, wbp_b, wbr_b, wout_b, ln1_g, ln1_b = wts
    const2 = lambda b, j: (layer, 0, 0)
    return pl.pallas_call(
        functools.partial(_mix_kernel, tt=tt, pos0=pos0, n_tiles=n_tiles),
        grid=(bsz, n_tiles),
        in_specs=[
            pl.BlockSpec((1, tt, D_IN), lambda b, j: (b, j, 0)),
            pl.BlockSpec((1, tt, D_MODEL), lambda b, j: (b, j, 0)),
            pl.BlockSpec((1, 1, D_MODEL), lambda b, j: (b, 0, 0)),
            pl.BlockSpec((tt, LANES), lambda b, j: (j, 0)),
            pl.BlockSpec((tt, LANES), lambda b, j: (j, 0)),
            pl.BlockSpec((1, HIST_ROWS, D_POOL), lambda b, j: (b, 0, 0)),
            pl.BlockSpec((1, N_RET_HEADS, DK_RET, DK_RET), lambda b, j: (b, 0, 0, 0)),
            pl.BlockSpec((None, len(POOL_WINDOWS), POOL_GROUP, POOL_GROUP),
                         lambda b, j: (layer, 0, 0, 0)),
            pl.BlockSpec((None, 1, D_POOL), const2),
            pl.BlockSpec((None, D_POOL, D_MODEL), const2),
            pl.BlockSpec((None, D_RET, D_MODEL), const2),
            pl.BlockSpec((None, D_MODEL, D_MODEL), const2),
            pl.BlockSpec((None, 1, D_MODEL), const2),
            pl.BlockSpec((None, 1, D_MODEL), const2),
        ],
        out_specs=[
            pl.BlockSpec((1, tt, D_MODEL), lambda b, j: (b, j, 0)),
            pl.BlockSpec((1, N_RET_HEADS, DK_RET, DK_RET), lambda b, j: (b, 0, 0, 0)),
        ],
        out_shape=[
            jax.ShapeDtypeStruct((bsz, seq, D_MODEL), F32),
            jax.ShapeDtypeStruct((bsz, N_RET_HEADS, DK_RET, DK_RET), F32),
        ],
        scratch_shapes=[
            pltpu.VMEM((N_RET_HEADS, DK_RET, DK_RET), F32),
            pltpu.VMEM((HIST_ROWS + tt, D_POOL), F32),
            pltpu.VMEM((N_RET_HEADS, tt, tt), F32),
            pltpu.VMEM((N_RET_HEADS, tt, LANES), F32),
            pltpu.VMEM((N_RET_HEADS, tt, LANES), F32),
            pltpu.VMEM((tt, D_RET), BF16),
            pltpu.VMEM((tt, D_POOL), BF16),
        ],
        compiler_params=_params("arbitrary", "arbitrary"),
        name="mix",
    )(proj, x, g1, cos2, sin2, hist16, s0, wgrp_b, pscale, wbp_b, wbr_b, wout_b, ln1_g, ln1_b)


def _top16_rows(s):
    rows = []
    cur = s
    for i in range(PEER_TOPK):
        m = jnp.max(cur, axis=0, keepdims=True)
        rows.append(m)
        if i + 1 < PEER_TOPK:
            cur = jnp.where(cur == m, NEG_INF, cur)
    return rows


def _stack_rows(rows, tt):
    ridx = lax.broadcasted_iota(jnp.int32, (PEER_TOPK, tt), 0)
    out = jnp.zeros((PEER_TOPK, tt), F32)
    for i, r in enumerate(rows):
        out = jnp.where(ridx == i, r, out)
    return out


def _activations(u_rows, h2_scr, s_dst, rows, n_lg):
    s_val = lax.dot_general(u_rows, h2_scr[...], (((1,), (1,)), ((), ())),
                            preferred_element_type=F32)
    for lg in range(n_lg):
        s_dst[lg, rows, :] = s_val[:, lg * LANES:(lg + 1) * LANES]


def _peer_kernel(x1_ref, sc_ref, sh_ref, g2_ref, wqt_ref, keys_ref, u_ref, vt_ref,
                 lng_ref, lnb_ref, o_ref,
                 h2_scr, qt_scr, c_scr, e1_scr, s2_scr, e2_scr, cb_scr, eb_scr, s_a, s_b, ga_scr, acc_scr,
                 *, tt, na, n_chunks):
    e = pl.program_id(1)
    n_lg = tt // LANES

    @pl.when(e == 0)
    def _prep():
        h2 = (_ln(x1_ref[...]) * (1.0 + sc_ref[0]) + sh_ref[0]).astype(BF16)
        h2_scr[...] = h2
        qt = lax.dot_general(wqt_ref[...], h2, (((1,), (1,)), ((), ())),
                             preferred_element_type=F32)
        qt_scr[...] = qt.astype(BF16)

        def head(h, carry):
            base = pl.multiple_of(h * PEER_DQ, PEER_DQ)
            s1 = jnp.dot(keys_ref[0], qt_scr[pl.ds(base, PEER_DHALF), :],
                         preferred_element_type=F32)
            s2 = jnp.dot(keys_ref[1], qt_scr[pl.ds(base + PEER_DHALF, PEER_DHALF), :],
                         preferred_element_type=F32)
            r1 = _top16_rows(s1)
            r2 = _top16_rows(s2)
            v1 = _stack_rows(r1, tt)
            v2 = _stack_rows(r2, tt)
            ridx = lax.broadcasted_iota(jnp.int32, (PEER_TOPK, tt), 0)
            half = PEER_TOPK // 2
            cand = [jnp.where(ridx < PEER_TOPK // (i + 1), r1[i] + v2, NEG_INF)
                    for i in range(half)]
            cand.append(jnp.where(ridx >= half, v1 + r2[0], NEG_INF))
            cur = cand
            tau = None
            for it in range(PEER_TOPK):
                mx = functools.reduce(jnp.maximum, cur)
                tau = jnp.max(mx, axis=0, keepdims=True)
                if it + 1 < PEER_TOPK:
                    cur = [jnp.where(c == tau, NEG_INF, c) for c in cur]
            top = r1[0] + r2[0]
            zsum = jnp.zeros((PEER_TOPK, tt), F32)
            for c in cand:
                zsum = zsum + jnp.where(c >= tau, jnp.exp(c - top), 0.0)
            z = jnp.sum(zsum, axis=0, keepdims=True)
            c_scr[h] = tau - s1
            e1_scr[h] = jnp.exp(s1 - r1[0])
            e2 = jnp.exp(s2 - r2[0]) / z
            for lg in range(n_lg):
                ls = slice(lg * LANES, (lg + 1) * LANES)
                s2_scr[h, lg] = s2[:, ls].reshape(KEY_TILES, SUBLANES, LANES)
                e2_scr[h, lg] = e2[:, ls].reshape(KEY_TILES, SUBLANES, LANES)
            return carry

        lax.fori_loop(0, PEER_HEADS, head, 0)

        acc_scr[...] = jnp.zeros_like(acc_scr)
        _activations(u_ref[...], h2_scr, s_a, slice(None), n_lg)

    def step(s_w, s_r):
        a0 = pl.multiple_of((e - 1) * na, na)
        for h in range(PEER_HEADS):
            cslab = c_scr[h, pl.ds(a0, na), :]
            eslab = e1_scr[h, pl.ds(a0, na), :]
            for r in range(na):
                cb = jnp.broadcast_to(cslab[r:r + 1, :], (SUBLANES, tt))
                eb = jnp.broadcast_to(eslab[r:r + 1, :], (SUBLANES, tt))
                for lg in range(n_lg):
                    cb_scr[h, r, lg] = cb[:, lg * LANES:(lg + 1) * LANES]
                    eb_scr[h, r, lg] = eb[:, lg * LANES:(lg + 1) * LANES]

        halves = [tuple(range(lo, min(lo + 2, n_lg))) for lo in range(0, n_lg, 2)]
        groups = [(half, ai, lg) for half in halves for ai in range(na) for lg in half]
        per_piece = len(groups) // ACT_PIECES
        rows_per = na * PEER_NKEYS // ACT_PIECES
        ga = None
        for gi, (half, ai, lg) in enumerate(groups):
            if gi % per_piece == 0:
                p = gi // per_piece
                start = p * rows_per
                if ga is not None:
                    probe = jnp.max(ga[0:SUBLANES, :])
                    start = start - jnp.where(probe != probe, start, 0)
                start = pl.multiple_of(start, PACKED_ROWS)
                _activations(u_ref[pl.ds(start, rows_per), :], h2_scr, s_w,
                             pl.ds(start, rows_per), n_lg)
            gate = jnp.zeros((KEY_TILES, SUBLANES, LANES), F32)
            for h in range(PEER_HEADS):
                hit = s2_scr[h, lg] >= cb_scr[h, ai, lg][None]
                gate = gate + jnp.where(hit, e2_scr[h, lg], 0.0) * eb_scr[h, ai, lg][None]
            act = _gelu_tanh(s_r[lg, ai * PEER_NKEYS:(ai + 1) * PEER_NKEYS, :])
            ga = gate.reshape(PEER_NKEYS, LANES) * act
            ga_scr[ai * PEER_NKEYS:(ai + 1) * PEER_NKEYS, lg * LANES:(lg + 1) * LANES] = ga.astype(BF16)
            if ai == na - 1 and lg == half[-1]:
                hs = slice(half[0] * LANES, (half[-1] + 1) * LANES)
                acc_scr[:, hs] += jnp.dot(vt_ref[...], ga_scr[:, hs], preferred_element_type=F32)

    @pl.when((e > 0) & (e % 2 == 1))
    def _odd():
        step(s_b, s_a)

    @pl.when((e > 0) & (e % 2 == 0))
    def _even():
        step(s_a, s_b)

    @pl.when(e == n_chunks)
    def _fin():
        y = acc_scr[...].T
        o_ref[...] = _ln(ALPHA * x1_ref[...] + g2_ref[0] * y) * lng_ref[...] + lnb_ref[...]


def _peer(x1_flat, sc, sh, g2, wts, layer, tt, per_token, tiles_per_seq):
    t = x1_flat.shape[0]
    wqt_b, keys_b, u_b, vt_b, ln2_g, ln2_b = wts
    na = 8
    ec = na * PEER_NKEYS
    n_chunks = PEER_N // ec
    n_lg = tt // LANES
    mod = _mod_spec(per_token, tt, tiles_per_seq)
    return pl.pallas_call(
        functools.partial(_peer_kernel, tt=tt, na=na, n_chunks=n_chunks),
        grid=(t // tt, n_chunks + 1),
        in_specs=[
            pl.BlockSpec((tt, D_MODEL), lambda i, e: (i, 0)),
            mod, mod, mod,
            pl.BlockSpec((None, PEER_HEADS * PEER_DQ, D_MODEL), lambda i, e: (layer, 0, 0)),
            pl.BlockSpec((None, 2, PEER_NKEYS, PEER_DHALF), lambda i, e: (layer, 0, 0, 0)),
            pl.BlockSpec((None, ec, D_MODEL),
                         lambda i, e: (layer, jnp.minimum(e, n_chunks - 1), 0)),
            pl.BlockSpec((None, D_MODEL, ec), lambda i, e: (layer, 0, jnp.maximum(e - 1, 0))),
            pl.BlockSpec((None, 1, D_MODEL), lambda i, e: (layer, 0, 0)),
            pl.BlockSpec((None, 1, D_MODEL), lambda i, e: (layer, 0, 0)),
        ],
        out_specs=pl.BlockSpec((tt, D_MODEL), lambda i, e: (i, 0)),
        out_shape=jax.ShapeDtypeStruct((t, D_MODEL), F32),
        scratch_shapes=[
            pltpu.VMEM((tt, D_MODEL), BF16),
            pltpu.VMEM((PEER_HEADS * PEER_DQ, tt), BF16),
            pltpu.VMEM((PEER_HEADS, PEER_NKEYS, tt), F32),
            pltpu.VMEM((PEER_HEADS, PEER_NKEYS, tt), F32),
            pltpu.VMEM((PEER_HEADS, n_lg, KEY_TILES, SUBLANES, LANES), F32),
            pltpu.VMEM((PEER_HEADS, n_lg, KEY_TILES, SUBLANES, LANES), F32),
            pltpu.VMEM((PEER_HEADS, na, n_lg, SUBLANES, LANES), F32),
            pltpu.VMEM((PEER_HEADS, na, n_lg, SUBLANES, LANES), F32),
            pltpu.VMEM((n_lg, ec, LANES), F32),
            pltpu.VMEM((n_lg, ec, LANES), F32),
            pltpu.VMEM((ec, tt), BF16),
            pltpu.VMEM((D_MODEL, tt), F32),
        ],
        compiler_params=_params("arbitrary", "arbitrary"),
        name="peer",
    )(x1_flat, sc, sh, g2, wqt_b, keys_b, u_b, vt_b, ln2_g, ln2_b)


def _rope_tables(pos0, seq):
    half = DK_RET // 2
    inv = ROPE_BASE ** (-jnp.arange(half, dtype=F32) / half)
    ang = (pos0 + jnp.arange(seq)).astype(F32)[:, None] * inv[None, :]
    cos = jnp.cos(ang)
    sin = jnp.sin(ang)
    return jnp.concatenate([cos, cos], axis=-1), jnp.concatenate([-sin, sin], axis=-1)


def _trunk(x, mods, pool_hist, ret_state, pos0, wts, tiles):
    bsz, seq, _ = x.shape
    tt_proj, tt_mix, tt_peer = tiles
    per_token = seq < tt_peer
    cos2, sin2 = _rope_tables(pos0, seq)
    hist16 = jnp.pad(pool_hist, ((0, 0), (0, 0), (HIST_ROWS - POOL_HIST, 0), (0, 0)))
    new_hist, new_state = [], []
    for l in range(DEPTH):
        chunks = [mods[l, :, k * D_MODEL:(k + 1) * D_MODEL] for k in range(6)]
        seq_rows = [c[:, None, :] for c in chunks]
        if per_token:
            tok_rows = [jnp.repeat(c, seq, axis=0)[None] for c in chunks]
        else:
            tok_rows = seq_rows
        sh1, sc1, _, sh2, sc2, g2 = tok_rows
        g1 = seq_rows[2]
        proj = _proj(x.reshape(bsz * seq, D_MODEL), sc1, sh1, wts["w_in"], l,
                     tt_proj, per_token, seq // tt_proj if not per_token else 1)
        proj = proj.reshape(bsz, seq, D_IN)
        x1, s_l = _mix(proj, x, g1, cos2, sin2, hist16[l], ret_state[l],
                       (wts["w_grp"], wts["pool_scale"], wts["w_bp"], wts["w_br"], wts["w_out"],
                        wts["ln1_g"], wts["ln1_b"]), l, tt_mix, pos0)
        x2 = _peer(x1.reshape(bsz * seq, D_MODEL), sc2, sh2, g2,
                   (wts["w_qt"], wts["keys"], wts["u"], wts["vt"], wts["ln2_g"], wts["ln2_b"]),
                   l, tt_peer, per_token, seq // tt_peer if not per_token else 1)
        x = x2.reshape(bsz, seq, D_MODEL)
        new_hist.append(proj[:, seq - POOL_HIST:, :D_POOL])
        new_state.append(s_l)
    return x, jnp.stack(new_hist), jnp.stack(new_state)


def kernel(x_prompt, x_sample, cache_pool, state_ret, c_prompt, c_sample, w_ada, b_ada, w_in,
           w_pool_grp, pool_scale, w_branch_pool, w_branch_ret, w_out, ln1_g, ln1_b, w_peer_q,
           peer_sub_keys, peer_u, peer_v, ln2_g, ln2_b):
    n_prompt = x_prompt.shape[0]
    n_sample = x_sample.shape[0]
    wts = {
        "w_in": w_in.astype(BF16),
        "w_grp": w_pool_grp.astype(BF16),
        "pool_scale": pool_scale[:, None, :],
        "w_bp": w_branch_pool.astype(BF16),
        "w_br": w_branch_ret.astype(BF16),
        "w_out": w_out.astype(BF16),
        "ln1_g": ln1_g[:, None, :],
        "ln1_b": ln1_b[:, None, :],
        "w_qt": jnp.swapaxes(w_peer_q, 1, 2).astype(BF16),
        "keys": peer_sub_keys.astype(BF16),
        "u": peer_u.astype(BF16),
        "vt": jnp.swapaxes(peer_v, 1, 2).astype(BF16),
        "ln2_g": ln2_g[:, None, :],
        "ln2_b": ln2_b[:, None, :],
    }
    n_seq = n_prompt + n_sample
    c_pad = jnp.pad(jnp.concatenate([c_prompt, c_sample], axis=0), ((0, 16 - n_seq), (0, 0)))
    mods = _ada(c_pad, w_ada, b_ada)
    zero_hist = jnp.zeros((DEPTH, n_prompt, POOL_HIST, D_POOL), x_prompt.dtype)
    zero_state = jnp.zeros((DEPTH, n_prompt, N_RET_HEADS, DK_RET, DK_RET), F32)
    y_p, pool_p, ret_p = _trunk(x_prompt, mods[:, :n_prompt], zero_hist, zero_state, 0, wts,
                                (1024, 256, 512))
    y_s, pool_s, ret_s = _trunk(x_sample, mods[:, n_prompt:n_seq], cache_pool, state_ret,
                                PAST_LEN, wts, (256, 32, 256))
    return (y_p, y_s, pool_p, ret_p.astype(state_ret.dtype), pool_s, ret_s.astype(state_ret.dtype))
```

```python
import functools
import math

import jax
import jax.numpy as jnp
from jax import lax
from jax.experimental import pallas as pl
from jax.experimental.pallas import tpu as pltpu

F32 = jnp.float32
BF16 = jnp.bfloat16

D_MODEL = 1024
DEPTH = 2
PAST_LEN = 2048
POOL_WINDOWS = (2, 4, 8, 16)
D_POOL = 512
POOL_GROUP = 128
POOL_HIST = 15
HIST_ROWS = 16
N_RET_HEADS = 8
DK_RET = 128
D_RET = N_RET_HEADS * DK_RET
RET_SCALE = DK_RET ** -0.5
ROPE_BASE = 10000.0
PEER_HEADS = 8
PEER_NKEYS = 128
PEER_N = PEER_NKEYS * PEER_NKEYS
PEER_DQ = 256
PEER_DHALF = 128
PEER_TOPK = 16
LN_EPS = 1e-5
ALPHA = (2 * DEPTH) ** 0.25
OFF_Q = D_POOL
OFF_K = OFF_Q + D_RET
OFF_V = OFF_K + D_RET
OFF_G = OFF_V + D_RET
OFF_BG = OFF_G + D_RET
D_IN = OFF_BG + 2 * D_MODEL
LOG_G = tuple(math.log(1.0 - 2.0 ** (-5.0 - h)) for h in range(N_RET_HEADS))
LANES = 128
SUBLANES = 8
KEY_TILES = PEER_NKEYS // SUBLANES
VMEM_LIMIT = 56 * 1024 * 1024
NEG_INF = float("-inf")


def _ln(x):
    mu = jnp.mean(x, axis=-1, keepdims=True)
    xc = x - mu
    var = jnp.mean(xc * xc, axis=-1, keepdims=True)
    return xc * lax.rsqrt(var + LN_EPS)


def _gelu_tanh(x):
    k0 = math.sqrt(2.0 / math.pi)
    half = 0.5 * x
    return half + half * jnp.tanh(x * (k0 + (k0 * 0.044715) * (x * x)))


def _params(*sem):
    return pltpu.CompilerParams(dimension_semantics=sem, vmem_limit_bytes=VMEM_LIMIT)


def _ada_kernel(c_ref, w_ref, b_ref, o_ref):
    c = c_ref[...]
    a = (c * jax.nn.sigmoid(c)).astype(BF16)
    o_ref[...] = jnp.dot(a, w_ref[...].astype(BF16), preferred_element_type=F32) + b_ref[...]


def _ada(c_pad, w_ada, b_ada):
    rows = c_pad.shape[0]
    n_out = w_ada.shape[-1]
    ct = 1536
    return pl.pallas_call(
        _ada_kernel,
        grid=(DEPTH, n_out // ct),
        in_specs=[
            pl.BlockSpec((rows, D_MODEL), lambda l, j: (0, 0)),
            pl.BlockSpec((None, D_MODEL, ct), lambda l, j: (l, 0, j)),
            pl.BlockSpec((None, 1, ct), lambda l, j: (l, 0, j)),
        ],
        out_specs=pl.BlockSpec((None, rows, ct), lambda l, j: (l, 0, j)),
        out_shape=jax.ShapeDtypeStruct((DEPTH, rows, n_out), F32),
        compiler_params=_params("arbitrary", "arbitrary"),
        name="ada",
    )(c_pad, w_ada, b_ada.reshape(DEPTH, 1, n_out))


def _mod_spec(per_token, tt, tiles_per_seq):
    if per_token:
        return pl.BlockSpec((1, tt, D_MODEL), lambda i, *_: (0, i, 0))
    return pl.BlockSpec((1, 1, D_MODEL), lambda i, *_: (i // tiles_per_seq, 0, 0))


def _proj_kernel(x_ref, sc_ref, sh_ref, w_ref, o_ref, h_scr):
    @pl.when(pl.program_id(1) == 0)
    def _():
        h = _ln(x_ref[...]) * (1.0 + sc_ref[0]) + sh_ref[0]
        h_scr[...] = h.astype(BF16)

    o_ref[...] = jnp.dot(h_scr[...], w_ref[...], preferred_element_type=F32)


def _proj(x_flat, sc, sh, w_in_b, layer, tt, per_token, tiles_per_seq):
    t = x_flat.shape[0]
    nt = D_IN // 4
    return pl.pallas_call(
        _proj_kernel,
        grid=(t // tt, D_IN // nt),
        in_specs=[
            pl.BlockSpec((tt, D_MODEL), lambda i, j: (i, 0)),
            _mod_spec(per_token, tt, tiles_per_seq),
            _mod_spec(per_token, tt, tiles_per_seq),
            pl.BlockSpec((None, D_MODEL, nt), lambda i, j: (layer, 0, j)),
        ],
        out_specs=pl.BlockSpec((tt, nt), lambda i, j: (i, j)),
        out_shape=jax.ShapeDtypeStruct((t, D_IN), F32),
        scratch_shapes=[pltpu.VMEM((tt, D_MODEL), BF16)],
        compiler_params=_params("arbitrary", "arbitrary"),
        name="proj",
    )(x_flat, sc, sh, w_in_b)


def _mix_kernel(proj_ref, x_ref, g1_ref, cos_ref, sin_ref, hist_ref, s0_ref,
                wgrp_ref, pscale_ref, wbp_ref, wbr_ref, wout_ref, lng_ref, lnb_ref,
                x1_ref, sfin_ref,
                state_scr, ext_scr, decay_scr, xi_scr, zeta_scr, retg_scr, pool_scr,
                *, tt, pos0, n_tiles):
    j = pl.program_id(1)

    @pl.when(j == 0)
    def _init():
        state_scr[...] = s0_ref[0]
        ext_scr[0:HIST_ROWS, :] = hist_ref[0]
        row = lax.broadcasted_iota(jnp.int32, (tt, tt), 0)
        col = lax.broadcasted_iota(jnp.int32, (tt, tt), 1)
        causal = row >= col
        diff = jnp.where(causal, (row - col).astype(F32), 0.0)
        rowl = lax.broadcasted_iota(jnp.int32, (tt, LANES), 0).astype(F32)
        for h in range(N_RET_HEADS):
            decay_scr[h] = jnp.where(causal, jnp.exp(LOG_G[h] * diff), 0.0)
            xi_scr[h] = jnp.exp(LOG_G[h] * (rowl + 1.0))
            zeta_scr[h] = jnp.exp(LOG_G[h] * (tt - 1.0 - rowl))

    p = proj_ref[0, :, 0:D_POOL]
    ext_scr[HIST_ROWS:HIST_ROWS + tt, :] = p
    pos = pos0 + j * tt + lax.broadcasted_iota(jnp.int32, (tt, POOL_GROUP), 0)
    for g, w in enumerate(POOL_WINDOWS):
        cs = slice(g * POOL_GROUP, (g + 1) * POOL_GROUP)
        ws = ext_scr[HIST_ROWS:HIST_ROWS + tt, cs]
        for d in range(1, w):
            ws = ws + ext_scr[HIST_ROWS - d:HIST_ROWS - d + tt, cs]
        cnt = jnp.minimum(pos + 1, w).astype(F32)
        pooled = ws / cnt - p[:, cs]
        mixed = jnp.dot(pooled.astype(BF16), wgrp_ref[g], preferred_element_type=F32)
        pool_scr[:, cs] = (mixed * pscale_ref[:, cs]).astype(BF16)
    ext_scr[0:HIST_ROWS, :] = ext_scr[tt:tt + HIST_ROWS, :]

    cosv = cos_ref[...]
    sinv = sin_ref[...]
    for h in range(N_RET_HEADS):
        hs = slice(h * DK_RET, (h + 1) * DK_RET)
        q = proj_ref[0, :, OFF_Q + h * DK_RET:OFF_Q + (h + 1) * DK_RET]
        k = proj_ref[0, :, OFF_K + h * DK_RET:OFF_K + (h + 1) * DK_RET]
        v = proj_ref[0, :, OFF_V + h * DK_RET:OFF_V + (h + 1) * DK_RET].astype(BF16)
        g = proj_ref[0, :, OFF_G + h * DK_RET:OFF_G + (h + 1) * DK_RET]
        qr = q * cosv + pltpu.roll(q, DK_RET // 2, 1) * sinv
        kr = (k * cosv + pltpu.roll(k, DK_RET // 2, 1) * sinv) * RET_SCALE
        qb = qr.astype(BF16)
        scores = lax.dot_general(qb, kr.astype(BF16), (((1,), (1,)), ((), ())),
                                 preferred_element_type=F32) * decay_scr[h]
        intra = jnp.dot(scores.astype(BF16), v, preferred_element_type=F32)
        s_prev = state_scr[h]
        cross = jnp.dot(qb, s_prev.astype(BF16), preferred_element_type=F32) * xi_scr[h]
        kz = (kr * zeta_scr[h]).astype(BF16)
        kv = lax.dot_general(kz, v, (((0,), (0,)), ((), ())), preferred_element_type=F32)
        state_scr[h] = math.exp(LOG_G[h] * tt) * s_prev + kv
        retg_scr[:, hs] = (_ln(intra + cross) * (g * jax.nn.sigmoid(g))).astype(BF16)

    bgp = jax.nn.sigmoid(proj_ref[0, :, OFF_BG:OFF_BG + D_MODEL])
    bgr = jax.nn.sigmoid(proj_ref[0, :, OFF_BG + D_MODEL:OFF_BG + 2 * D_MODEL])
    pb = jnp.dot(pool_scr[...], wbp_ref[...], preferred_element_type=F32)
    rb = jnp.dot(retg_scr[...], wbr_ref[...], preferred_element_type=F32)
    merged = (bgp * pb + bgr * rb).astype(BF16)
    z = jnp.dot(merged, wout_ref[...], preferred_element_type=F32)
    x1_ref[0] = _ln(ALPHA * x_ref[0] + g1_ref[0] * z) * lng_ref[...] + lnb_ref[...]

    @pl.when(j == n_tiles - 1)
    def _fin():
        sfin_ref[0] = state_scr[...]


def _mix(proj, x, g1, cos2, sin2, hist16, s0, wts, layer, tt, pos0):
    bsz, seq, _ = x.shape
    n_tiles = seq // tt
    wgrp_b, pscale, wbp_b, wbr_b, wout_b, ln1_g, ln1_b = wts
    const2 = lambda b, j: (layer, 0, 0)
    return pl.pallas_call(
        functools.partial(_mix_kernel, tt=tt, pos0=pos0, n_tiles=n_tiles),
        grid=(bsz, n_tiles),
        in_specs=[
            pl.BlockSpec((1, tt, D_IN), lambda b, j: (b, j, 0)),
            pl.BlockSpec((1, tt, D_MODEL), lambda b, j: (b, j, 0)),
            pl.BlockSpec((1, 1, D_MODEL), lambda b, j: (b, 0, 0)),
            pl.BlockSpec((tt, LANES), lambda b, j: (j, 0)),
            pl.BlockSpec((tt, LANES), lambda b, j: (j, 0)),
            pl.BlockSpec((1, HIST_ROWS, D_POOL), lambda b, j: (b, 0, 0)),
            pl.BlockSpec((1, N_RET_HEADS, DK_RET, DK_RET), lambda b, j: (b, 0, 0, 0)),
            pl.BlockSpec((None, len(POOL_WINDOWS), POOL_GROUP, POOL_GROUP),
                         lambda b, j: (layer, 0, 0, 0)),
            pl.BlockSpec((None, 1, D_POOL), const2),
            pl.BlockSpec((None, D_POOL, D_MODEL), const2),
            pl.BlockSpec((None, D_RET, D_MODEL), const2),
            pl.BlockSpec((None, D_MODEL, D_MODEL), const2),
            pl.BlockSpec((None, 1, D_MODEL), const2),
            pl.BlockSpec((None, 1, D_MODEL), const2),
        ],
        out_specs=[
            pl.BlockSpec((1, tt, D_MODEL), lambda b, j: (b, j, 0)),
            pl.BlockSpec((1, N_RET_HEADS, DK_RET, DK_RET), lambda b, j: (b, 0, 0, 0)),
        ],
        out_shape=[
            jax.ShapeDtypeStruct((bsz, seq, D_MODEL), F32),
            jax.ShapeDtypeStruct((bsz, N_RET_HEADS, DK_RET, DK_RET), F32),
        ],
        scratch_shapes=[
            pltpu.VMEM((N_RET_HEADS, DK_RET, DK_RET), F32),
            pltpu.VMEM((HIST_ROWS + tt, D_POOL), F32),
            pltpu.VMEM((N_RET_HEADS, tt, tt), F32),
            pltpu.VMEM((N_RET_HEADS, tt, LANES), F32),
            pltpu.VMEM((N_RET_HEADS, tt, LANES), F32),
            pltpu.VMEM((tt, D_RET), BF16),
            pltpu.VMEM((tt, D_POOL), BF16),
        ],
        compiler_params=_params("arbitrary", "arbitrary"),
        name="mix",
    )(proj, x, g1, cos2, sin2, hist16, s0, wgrp_b, pscale, wbp_b, wbr_b, wout_b, ln1_g, ln1_b)


def _top16_rows(s):
    rows = []
    cur = s
    for i in range(PEER_TOPK):
        m = jnp.max(cur, axis=0, keepdims=True)
        rows.append(m)
        if i + 1 < PEER_TOPK:
            cur = jnp.where(cur == m, NEG_INF, cur)
    return rows


def _stack_rows(rows, tt):
    ridx = lax.broadcasted_iota(jnp.int32, (PEER_TOPK, tt), 0)
    out = jnp.zeros((PEER_TOPK, tt), F32)
    for i, r in enumerate(rows):
        out = jnp.where(ridx == i, r, out)
    return out


def _peer_kernel(x1_ref, sc_ref, sh_ref, g2_ref, wqt_ref, keys_ref, u_ref, vt_ref,
                 lng_ref, lnb_ref, o_ref,
                 h2_scr, qt_scr, c_scr, e1_scr, s2_scr, e2_scr, cb_scr, eb_scr, s_scr, ga_scr, acc_scr,
                 *, tt, na, n_chunks):
    e = pl.program_id(1)
    n_lg = tt // LANES

    @pl.when(e == 0)
    def _prep():
        h2 = (_ln(x1_ref[...]) * (1.0 + sc_ref[0]) + sh_ref[0]).astype(BF16)
        h2_scr[...] = h2
        qt = lax.dot_general(wqt_ref[...], h2, (((1,), (1,)), ((), ())),
                             preferred_element_type=F32)
        qt_scr[...] = qt.astype(BF16)

        def head(h, carry):
            base = pl.multiple_of(h * PEER_DQ, PEER_DQ)
            s1 = jnp.dot(keys_ref[0], qt_scr[pl.ds(base, PEER_DHALF), :],
                         preferred_element_type=F32)
            s2 = jnp.dot(keys_ref[1], qt_scr[pl.ds(base + PEER_DHALF, PEER_DHALF), :],
                         preferred_element_type=F32)
            r1 = _top16_rows(s1)
            r2 = _top16_rows(s2)
            v1 = _stack_rows(r1, tt)
            v2 = _stack_rows(r2, tt)
            ridx = lax.broadcasted_iota(jnp.int32, (PEER_TOPK, tt), 0)
            half = PEER_TOPK // 2
            cand = [jnp.where(ridx < PEER_TOPK // (i + 1), r1[i] + v2, NEG_INF)
                    for i in range(half)]
            cand.append(jnp.where(ridx >= half, v1 + r2[0], NEG_INF))
            cur = cand
            tau = None
            for it in range(PEER_TOPK):
                mx = functools.reduce(jnp.maximum, cur)
                tau = jnp.max(mx, axis=0, keepdims=True)
                if it + 1 < PEER_TOPK:
                    cur = [jnp.where(c == tau, NEG_INF, c) for c in cur]
            top = r1[0] + r2[0]
            zsum = jnp.zeros((PEER_TOPK, tt), F32)
            for c in cand:
                zsum = zsum + jnp.where(c >= tau, jnp.exp(c - top), 0.0)
            z = jnp.sum(zsum, axis=0, keepdims=True)
            c_scr[h] = tau - s1
            e1_scr[h] = jnp.exp(s1 - r1[0])
            e2 = jnp.exp(s2 - r2[0]) / z
            for lg in range(n_lg):
                ls = slice(lg * LANES, (lg + 1) * LANES)
                s2_scr[h, lg] = s2[:, ls].reshape(KEY_TILES, SUBLANES, LANES)
                e2_scr[h, lg] = e2[:, ls].reshape(KEY_TILES, SUBLANES, LANES)
            return carry

        lax.fori_loop(0, PEER_HEADS, head, 0)

    s_val = lax.dot_general(u_ref[...], h2_scr[...], (((1,), (1,)), ((), ())),
                            preferred_element_type=F32)
    for lg in range(n_lg):
        s_scr[lg] = s_val[:, lg * LANES:(lg + 1) * LANES]

    a0 = pl.multiple_of(e * na, na)
    for h in range(PEER_HEADS):
        cslab = c_scr[h, pl.ds(a0, na), :]
        eslab = e1_scr[h, pl.ds(a0, na), :]
        for r in range(na):
            cb = jnp.broadcast_to(cslab[r:r + 1, :], (SUBLANES, tt))
            eb = jnp.broadcast_to(eslab[r:r + 1, :], (SUBLANES, tt))
            for lg in range(n_lg):
                cb_scr[h, r, lg] = cb[:, lg * LANES:(lg + 1) * LANES]
                eb_scr[h, r, lg] = eb[:, lg * LANES:(lg + 1) * LANES]

    for lg in range(n_lg):
        for ai in range(na):
            gate = jnp.zeros((KEY_TILES, SUBLANES, LANES), F32)
            for h in range(PEER_HEADS):
                hit = s2_scr[h, lg] >= cb_scr[h, ai, lg][None]
                gate = gate + jnp.where(hit, e2_scr[h, lg], 0.0) * eb_scr[h, ai, lg][None]
            act = _gelu_tanh(s_scr[lg, ai * PEER_NKEYS:(ai + 1) * PEER_NKEYS, :])
            ga = gate.reshape(PEER_NKEYS, LANES) * act
            ga_scr[ai * PEER_NKEYS:(ai + 1) * PEER_NKEYS, lg * LANES:(lg + 1) * LANES] = ga.astype(BF16)

    contrib = jnp.dot(vt_ref[...], ga_scr[...], preferred_element_type=F32)

    @pl.when(e == 0)
    def _():
        acc_scr[...] = contrib

    @pl.when(e > 0)
    def _():
        acc_scr[...] += contrib

    @pl.when(e == n_chunks - 1)
    def _fin():
        y = acc_scr[...].T
        o_ref[...] = _ln(ALPHA * x1_ref[...] + g2_ref[0] * y) * lng_ref[...] + lnb_ref[...]


def _peer(x1_flat, sc, sh, g2, wts, layer, tt, per_token, tiles_per_seq):
    t = x1_flat.shape[0]
    wqt_b, keys_b, u_b, vt_b, ln2_g, ln2_b = wts
    na = 8
    ec = na * PEER_NKEYS
    n_chunks = PEER_N // ec
    n_lg = tt // LANES
    mod = _mod_spec(per_token, tt, tiles_per_seq)
    return pl.pallas_call(
        functools.partial(_peer_kernel, tt=tt, na=na, n_chunks=n_chunks),
        grid=(t // tt, n_chunks),
        in_specs=[
            pl.BlockSpec((tt, D_MODEL), lambda i, e: (i, 0)),
            mod, mod, mod,
            pl.BlockSpec((None, PEER_HEADS * PEER_DQ, D_MODEL), lambda i, e: (layer, 0, 0)),
            pl.BlockSpec((None, 2, PEER_NKEYS, PEER_DHALF), lambda i, e: (layer, 0, 0, 0)),
            pl.BlockSpec((None, ec, D_MODEL), lambda i, e: (layer, e, 0)),
            pl.BlockSpec((None, D_MODEL, ec), lambda i, e: (layer, 0, e)),
            pl.BlockSpec((None, 1, D_MODEL), lambda i, e: (layer, 0, 0)),
            pl.BlockSpec((None, 1, D_MODEL), lambda i, e: (layer, 0, 0)),
        ],
        out_specs=pl.BlockSpec((tt, D_MODEL), lambda i, e: (i, 0)),
        out_shape=jax.ShapeDtypeStruct((t, D_MODEL), F32),
        scratch_shapes=[
            pltpu.VMEM((tt, D_MODEL), BF16),
            pltpu.VMEM((PEER_HEADS * PEER_DQ, tt), BF16),
            pltpu.VMEM((PEER_HEADS, PEER_NKEYS, tt), F32),
            pltpu.VMEM((PEER_HEADS, PEER_NKEYS, tt), F32),
            pltpu.VMEM((PEER_HEADS, n_lg, KEY_TILES, SUBLANES, LANES), F32),
            pltpu.VMEM((PEER_HEADS, n_lg, KEY_TILES, SUBLANES, LANES), F32),
            pltpu.VMEM((PEER_HEADS, na, n_lg, SUBLANES, LANES), F32),
            pltpu.VMEM((PEER_HEADS, na, n_lg, SUBLANES, LANES), F32),
            pltpu.VMEM((n_lg, ec, LANES), F32),
            pltpu.VMEM((ec, tt), BF16),
            pltpu.VMEM((D_MODEL, tt), F32),
        ],
        compiler_params=_params("arbitrary", "arbitrary"),
        name="peer",
    )(x1_flat, sc, sh, g2, wqt_b, keys_b, u_b, vt_b, ln2_g, ln2_b)


def _rope_tables(pos0, seq):
    half = DK_RET // 2
    inv = ROPE_BASE ** (-jnp.arange(half, dtype=F32) / half)
    ang = (pos0 + jnp.arange(seq)).astype(F32)[:, None] * inv[None, :]
    cos = jnp.cos(ang)
    sin = jnp.sin(ang)
    return jnp.concatenate([cos, cos], axis=-1), jnp.concatenate([-sin, sin], axis=-1)


def _trunk(x, mods, pool_hist, ret_state, pos0, wts, tiles):
    bsz, seq, _ = x.shape
    tt_proj, tt_mix, tt_peer = tiles
    per_token = seq < tt_peer
    cos2, sin2 = _rope_tables(pos0, seq)
    hist16 = jnp.pad(pool_hist, ((0, 0), (0, 0), (HIST_ROWS - POOL_HIST, 0), (0, 0)))
    new_hist, new_state = [], []
    for l in range(DEPTH):
        chunks = [mods[l, :, k * D_MODEL:(k + 1) * D_MODEL] for k in range(6)]
        seq_rows = [c[:, None, :] for c in chunks]
        if per_token:
            tok_rows = [jnp.repeat(c, seq, axis=0)[None] for c in chunks]
        else:
            tok_rows = seq_rows
        sh1, sc1, _, sh2, sc2, g2 = tok_rows
        g1 = seq_rows[2]
        proj = _proj(x.reshape(bsz * seq, D_MODEL), sc1, sh1, wts["w_in"], l,
                     tt_proj, per_token, seq // tt_proj if not per_token else 1)
        proj = proj.reshape(bsz, seq, D_IN)
        x1, s_l = _mix(proj, x, g1, cos2, sin2, hist16[l], ret_state[l],
                       (wts["w_grp"], wts["pool_scale"], wts["w_bp"], wts["w_br"], wts["w_out"],
                        wts["ln1_g"], wts["ln1_b"]), l, tt_mix, pos0)
        x2 = _peer(x1.reshape(bsz * seq, D_MODEL), sc2, sh2, g2,
                   (wts["w_qt"], wts["keys"], wts["u"], wts["vt"], wts["ln2_g"], wts["ln2_b"]),
                   l, tt_peer, per_token, seq // tt_peer if not per_token else 1)
        x = x2.reshape(bsz, seq, D_MODEL)
        new_hist.append(proj[:, seq - POOL_HIST:, :D_POOL])
        new_state.append(s_l)
    return x, jnp.stack(new_hist), jnp.stack(new_state)


def kernel(x_prompt, x_sample, cache_pool, state_ret, c_prompt, c_sample, w_ada, b_ada, w_in,
           w_pool_grp, pool_scale, w_branch_pool, w_branch_ret, w_out, ln1_g, ln1_b, w_peer_q,
           peer_sub_keys, peer_u, peer_v, ln2_g, ln2_b):
    n_prompt = x_prompt.shape[0]
    n_sample = x_sample.shape[0]
    wts = {
        "w_in": w_in.astype(BF16),
        "w_grp": w_pool_grp.astype(BF16),
        "pool_scale": pool_scale[:, None, :],
        "w_bp": w_branch_pool.astype(BF16),
        "w_br": w_branch_ret.astype(BF16),
        "w_out": w_out.astype(BF16),
        "ln1_g": ln1_g[:, None, :],
        "ln1_b": ln1_b[:, None, :],
        "w_qt": jnp.swapaxes(w_peer_q, 1, 2).astype(BF16),
        "keys": peer_sub_keys.astype(BF16),
        "u": peer_u.astype(BF16),
        "vt": jnp.swapaxes(peer_v, 1, 2).astype(BF16),
        "ln2_g": ln2_g[:, None, :],
        "ln2_b": ln2_b[:, None, :],
    }
    n_seq = n_prompt + n_sample
    c_pad = jnp.pad(jnp.concatenate([c_prompt, c_sample], axis=0), ((0, 16 - n_seq), (0, 0)))
    mods = _ada(c_pad, w_ada, b_ada)
    zero_hist = jnp.zeros((DEPTH, n_prompt, POOL_HIST, D_POOL), x_prompt.dtype)
    zero_state = jnp.zeros((DEPTH, n_prompt, N_RET_HEADS, DK_RET, DK_RET), F32)
    y_p, pool_p, ret_p = _trunk(x_prompt, mods[:, :n_prompt], zero_hist, zero_state, 0, wts,
                                (1024, 256, 512))
    y_s, pool_s, ret_s = _trunk(x_sample, mods[:, n_prompt:n_seq], cache_pool, state_ret,
                                PAST_LEN, wts, (256, 32, 256))
    return (y_p, y_s, pool_p, ret_p.astype(state_ret.dtype), pool_s, ret_s.astype(state_ret.dtype))
```

```python
import functools
import math

import jax
import jax.numpy as jnp
from jax import lax
from jax.experimental import pallas as pl
from jax.experimental.pallas import tpu as pltpu

F32 = jnp.float32
BF16 = jnp.bfloat16

D_MODEL = 1024
DEPTH = 2
PAST_LEN = 2048
POOL_WINDOWS = (2, 4, 8, 16)
D_POOL = 512
POOL_GROUP = 128
POOL_HIST = 15
HIST_ROWS = 16
N_RET_HEADS = 8
DK_RET = 128
D_RET = N_RET_HEADS * DK_RET
RET_SCALE = DK_RET ** -0.5
ROPE_BASE = 10000.0
PEER_HEADS = 8
PEER_NKEYS = 128
PEER_N = PEER_NKEYS * PEER_NKEYS
PEER_DQ = 256
PEER_DHALF = 128
PEER_TOPK = 16
LN_EPS = 1e-5
ALPHA = (2 * DEPTH) ** 0.25
OFF_Q = D_POOL
OFF_K = OFF_Q + D_RET
OFF_V = OFF_K + D_RET
OFF_G = OFF_V + D_RET
OFF_BG = OFF_G + D_RET
D_IN = OFF_BG + 2 * D_MODEL
LOG_G = tuple(math.log(1.0 - 2.0 ** (-5.0 - h)) for h in range(N_RET_HEADS))
LANES = 128
SUBLANES = 8
KEY_TILES = PEER_NKEYS // SUBLANES
PACKED_ROWS = 16
ACT_PIECES = 2
ACT_LAG = 1
VMEM_LIMIT = 56 * 1024 * 1024
NEG_INF = float("-inf")


def _ln(x):
    mu = jnp.mean(x, axis=-1, keepdims=True)
    xc = x - mu
    var = jnp.mean(xc * xc, axis=-1, keepdims=True)
    return xc * lax.rsqrt(var + LN_EPS)


def _gelu_tanh(x):
    k0 = math.sqrt(2.0 / math.pi)
    half = 0.5 * x
    return half + half * jnp.tanh(x * (k0 + (k0 * 0.044715) * (x * x)))


def _params(*sem):
    return pltpu.CompilerParams(dimension_semantics=sem, vmem_limit_bytes=VMEM_LIMIT)


def _ada_kernel(c_ref, w_ref, b_ref, o_ref):
    c = c_ref[...]
    a = (c * jax.nn.sigmoid(c)).astype(BF16)
    o_ref[...] = jnp.dot(a, w_ref[...].astype(BF16), preferred_element_type=F32) + b_ref[...]


def _ada(c_pad, w_ada, b_ada):
    rows = c_pad.shape[0]
    n_out = w_ada.shape[-1]
    ct = 1536
    return pl.pallas_call(
        _ada_kernel,
        grid=(DEPTH, n_out // ct),
        in_specs=[
            pl.BlockSpec((rows, D_MODEL), lambda l, j: (0, 0)),
            pl.BlockSpec((None, D_MODEL, ct), lambda l, j: (l, 0, j)),
            pl.BlockSpec((None, 1, ct), lambda l, j: (l, 0, j)),
        ],
        out_specs=pl.BlockSpec((None, rows, ct), lambda l, j: (l, 0, j)),
        out_shape=jax.ShapeDtypeStruct((DEPTH, rows, n_out), F32),
        compiler_params=_params("arbitrary", "arbitrary"),
        name="ada",
    )(c_pad, w_ada, b_ada.reshape(DEPTH, 1, n_out))


def _mod_spec(per_token, tt, tiles_per_seq):
    if per_token:
        return pl.BlockSpec((1, tt, D_MODEL), lambda i, *_: (0, i, 0))
    return pl.BlockSpec((1, 1, D_MODEL), lambda i, *_: (i // tiles_per_seq, 0, 0))


def _mix_kernel(x_ref, sc_ref, sh_ref, g1_ref, cos_ref, sin_ref, hist_ref, s0_ref, win_ref,
                wgrp_ref, pscale_ref, wbp_ref, wbr_ref, wout_ref, lng_ref, lnb_ref,
                x1_ref, sfin_ref, hout_ref,
                proj_ref, state_scr, ext_scr, decay_scr, xi_scr, zeta_scr, retg_scr, pool_scr,
                *, tt, pos0, n_tiles):
    j = pl.program_id(1)

    h_in = (_ln(x_ref[0]) * (1.0 + sc_ref[0]) + sh_ref[0]).astype(BF16)
    proj_ref[...] = jnp.dot(h_in, win_ref[...], preferred_element_type=F32)

    @pl.when(j == 0)
    def _init():
        state_scr[...] = s0_ref[0]
        ext_scr[0:HIST_ROWS, :] = hist_ref[0]
        row = lax.broadcasted_iota(jnp.int32, (tt, tt), 0)
        col = lax.broadcasted_iota(jnp.int32, (tt, tt), 1)
        causal = row >= col
        diff = jnp.where(causal, (row - col).astype(F32), 0.0)
        rowl = lax.broadcasted_iota(jnp.int32, (tt, LANES), 0).astype(F32)
        for h in range(N_RET_HEADS):
            decay_scr[h] = jnp.where(causal, jnp.exp(LOG_G[h] * diff), 0.0)
            xi_scr[h] = jnp.exp(LOG_G[h] * (rowl + 1.0))
            zeta_scr[h] = jnp.exp(LOG_G[h] * (tt - 1.0 - rowl))

    p = proj_ref[:,0:D_POOL]
    ext_scr[HIST_ROWS:HIST_ROWS + tt, :] = p
    pos = pos0 + j * tt + lax.broadcasted_iota(jnp.int32, (tt, POOL_GROUP), 0)
    for g, w in enumerate(POOL_WINDOWS):
        cs = slice(g * POOL_GROUP, (g + 1) * POOL_GROUP)
        ws = ext_scr[HIST_ROWS:HIST_ROWS + tt, cs]
        for d in range(1, w):
            ws = ws + ext_scr[HIST_ROWS - d:HIST_ROWS - d + tt, cs]
        cnt = jnp.minimum(pos + 1, w).astype(F32)
        pooled = ws / cnt - p[:, cs]
        mixed = jnp.dot(pooled.astype(BF16), wgrp_ref[g], preferred_element_type=F32)
        pool_scr[:, cs] = (mixed * pscale_ref[:, cs]).astype(BF16)
    ext_scr[0:HIST_ROWS, :] = ext_scr[tt:tt + HIST_ROWS, :]

    cosv = cos_ref[...]
    sinv = sin_ref[...]
    for h in range(N_RET_HEADS):
        hs = slice(h * DK_RET, (h + 1) * DK_RET)
        q = proj_ref[:,OFF_Q + h * DK_RET:OFF_Q + (h + 1) * DK_RET]
        k = proj_ref[:,OFF_K + h * DK_RET:OFF_K + (h + 1) * DK_RET]
        v = proj_ref[:,OFF_V + h * DK_RET:OFF_V + (h + 1) * DK_RET].astype(BF16)
        g = proj_ref[:,OFF_G + h * DK_RET:OFF_G + (h + 1) * DK_RET]
        qr = q * cosv + pltpu.roll(q, DK_RET // 2, 1) * sinv
        kr = (k * cosv + pltpu.roll(k, DK_RET // 2, 1) * sinv) * RET_SCALE
        qb = qr.astype(BF16)
        scores = lax.dot_general(qb, kr.astype(BF16), (((1,), (1,)), ((), ())),
                                 preferred_element_type=F32) * decay_scr[h]
        intra = jnp.dot(scores.astype(BF16), v, preferred_element_type=F32)
        s_prev = state_scr[h]
        cross = jnp.dot(qb, s_prev.astype(BF16), preferred_element_type=F32) * xi_scr[h]
        kz = (kr * zeta_scr[h]).astype(BF16)
        kv = lax.dot_general(kz, v, (((0,), (0,)), ((), ())), preferred_element_type=F32)
        state_scr[h] = math.exp(LOG_G[h] * tt) * s_prev + kv
        retg_scr[:, hs] = (_ln(intra + cross) * (g * jax.nn.sigmoid(g))).astype(BF16)

    bgp = jax.nn.sigmoid(proj_ref[:,OFF_BG:OFF_BG + D_MODEL])
    bgr = jax.nn.sigmoid(proj_ref[:,OFF_BG + D_MODEL:OFF_BG + 2 * D_MODEL])
    pb = jnp.dot(pool_scr[...], wbp_ref[...], preferred_element_type=F32)
    rb = jnp.dot(retg_scr[...], wbr_ref[...], preferred_element_type=F32)
    merged = (bgp * pb + bgr * rb).astype(BF16)
    z = jnp.dot(merged, wout_ref[...], preferred_element_type=F32)
    x1_ref[0] = _ln(ALPHA * x_ref[0] + g1_ref[0] * z) * lng_ref[...] + lnb_ref[...]

    @pl.when(j == n_tiles - 1)
    def _fin():
        sfin_ref[0] = state_scr[...]
        hout_ref[0] = ext_scr[0:HIST_ROWS, :]


def _mix(x, sc1, sh1, g1, cos2, sin2, hist16, s0, wts, layer, tt, pos0):
    bsz, seq, _ = x.shape
    n_tiles = seq // tt
    w_in_b, wgrp_b, pscale, wbp_b, wbr_b, wout_b, ln1_g, ln1_b = wts
    const2 = lambda b, j: (layer, 0, 0)
    once = pl.Buffered(1)
    seq_row = pl.BlockSpec((1, 1, D_MODEL), lambda b, j: (b, 0, 0))
    return pl.pallas_call(
        functools.partial(_mix_kernel, tt=tt, pos0=pos0, n_tiles=n_tiles),
        grid=(bsz, n_tiles),
        in_specs=[
            pl.BlockSpec((1, tt, D_MODEL), lambda b, j: (b, j, 0)),
            seq_row, seq_row, seq_row,
            pl.BlockSpec((tt, LANES), lambda b, j: (j, 0)),
            pl.BlockSpec((tt, LANES), lambda b, j: (j, 0)),
            pl.BlockSpec((1, HIST_ROWS, D_POOL), lambda b, j: (b, 0, 0)),
            pl.BlockSpec((1, N_RET_HEADS, DK_RET, DK_RET), lambda b, j: (b, 0, 0, 0)),
            pl.BlockSpec((None, D_MODEL, D_IN), const2, pipeline_mode=once),
            pl.BlockSpec((None, len(POOL_WINDOWS), POOL_GROUP, POOL_GROUP),
                         lambda b, j: (layer, 0, 0, 0)),
            pl.BlockSpec((None, 1, D_POOL), const2),
            pl.BlockSpec((None, D_POOL, D_MODEL), const2, pipeline_mode=once),
            pl.BlockSpec((None, D_RET, D_MODEL), const2, pipeline_mode=once),
            pl.BlockSpec((None, D_MODEL, D_MODEL), const2, pipeline_mode=once),
            pl.BlockSpec((None, 1, D_MODEL), const2),
            pl.BlockSpec((None, 1, D_MODEL), const2),
        ],
        out_specs=[
            pl.BlockSpec((1, tt, D_MODEL), lambda b, j: (b, j, 0)),
            pl.BlockSpec((1, N_RET_HEADS, DK_RET, DK_RET), lambda b, j: (b, 0, 0, 0)),
            pl.BlockSpec((1, HIST_ROWS, D_POOL), lambda b, j: (b, 0, 0)),
        ],
        out_shape=[
            jax.ShapeDtypeStruct((bsz, seq, D_MODEL), F32),
            jax.ShapeDtypeStruct((bsz, N_RET_HEADS, DK_RET, DK_RET), F32),
            jax.ShapeDtypeStruct((bsz, HIST_ROWS, D_POOL), F32),
        ],
        scratch_shapes=[
            pltpu.VMEM((tt, D_IN), F32),
            pltpu.VMEM((N_RET_HEADS, DK_RET, DK_RET), F32),
            pltpu.VMEM((HIST_ROWS + tt, D_POOL), F32),
            pltpu.VMEM((N_RET_HEADS, tt, tt), F32),
            pltpu.VMEM((N_RET_HEADS, tt, LANES), F32),
            pltpu.VMEM((N_RET_HEADS, tt, LANES), F32),
            pltpu.VMEM((tt, D_RET), BF16),
            pltpu.VMEM((tt, D_POOL), BF16),
        ],
        compiler_params=_params("arbitrary", "arbitrary"),
        name="mix",
    )(x, sc1, sh1, g1, cos2, sin2, hist16, s0, w_in_b, wgrp_b, pscale, wbp_b, wbr_b, wout_b,
      ln1_g, ln1_b)


def _top16_rows(s):
    rows = []
    cur = s
    for i in range(PEER_TOPK):
        m = jnp.max(cur, axis=0, keepdims=True)
        rows.append(m)
        if i + 1 < PEER_TOPK:
            cur = jnp.where(cur == m, NEG_INF, cur)
    return rows


def _stack_rows(rows, tt):
    ridx = lax.broadcasted_iota(jnp.int32, (PEER_TOPK, tt), 0)
    out = jnp.zeros((PEER_TOPK, tt), F32)
    for i, r in enumerate(rows):
        out = jnp.where(ridx == i, r, out)
    return out


def _activations(u_rows, h2_scr, s_dst, rows, n_lg):
    s_val = lax.dot_general(u_rows, h2_scr[...], (((1,), (1,)), ((), ())),
                            preferred_element_type=F32)
    for lg in range(n_lg):
        s_dst[lg, rows, :] = s_val[:, lg * LANES:(lg + 1) * LANES]


def _peer_kernel(x1_ref, sc_ref, sh_ref, g2_ref, wqt_ref, keys_ref, u_ref, vt_ref,
                 lng_ref, lnb_ref, o_ref,
                 h2_scr, qt_scr, c_scr, e1_scr, s2_scr, e2_scr, cb_scr, eb_scr, s_a, s_b, ga_scr, acc_scr,
                 *, tt, na, n_chunks):
    e = pl.program_id(1)
    n_lg = tt // LANES

    @pl.when(e == 0)
    def _prep():
        h2 = (_ln(x1_ref[...]) * (1.0 + sc_ref[0]) + sh_ref[0]).astype(BF16)
        h2_scr[...] = h2
        qt = lax.dot_general(wqt_ref[...], h2, (((1,), (1,)), ((), ())),
                             preferred_element_type=F32)
        qt_scr[...] = qt.astype(BF16)

        def head(h, carry):
            base = pl.multiple_of(h * PEER_DQ, PEER_DQ)
            s1 = jnp.dot(keys_ref[0], qt_scr[pl.ds(base, PEER_DHALF), :],
                         preferred_element_type=F32)
            s2 = jnp.dot(keys_ref[1], qt_scr[pl.ds(base + PEER_DHALF, PEER_DHALF), :],
                         preferred_element_type=F32)
            r1 = _top16_rows(s1)
            r2 = _top16_rows(s2)
            v1 = _stack_rows(r1, tt)
            v2 = _stack_rows(r2, tt)
            ridx = lax.broadcasted_iota(jnp.int32, (PEER_TOPK, tt), 0)
            half = PEER_TOPK // 2
            cand = [jnp.where(ridx < PEER_TOPK // (i + 1), r1[i] + v2, NEG_INF)
                    for i in range(half)]
            cand.append(jnp.where(ridx >= half, v1 + r2[0], NEG_INF))
            cur = cand
            tau = None
            for it in range(PEER_TOPK):
                mx = functools.reduce(jnp.maximum, cur)
                tau = jnp.max(mx, axis=0, keepdims=True)
                if it + 1 < PEER_TOPK:
                    cur = [jnp.where(c == tau, NEG_INF, c) for c in cur]
            top = r1[0] + r2[0]
            zsum = jnp.zeros((PEER_TOPK, tt), F32)
            for c in cand:
                zsum = zsum + jnp.where(c >= tau, jnp.exp(c - top), 0.0)
            z = jnp.sum(zsum, axis=0, keepdims=True)
            c_scr[h] = tau - s1
            e1_scr[h] = jnp.exp(s1 - r1[0])
            e2 = jnp.exp(s2 - r2[0]) / z
            for lg in range(n_lg):
                ls = slice(lg * LANES, (lg + 1) * LANES)
                s2_scr[h, lg] = s2[:, ls].reshape(KEY_TILES, SUBLANES, LANES)
                e2_scr[h, lg] = e2[:, ls].reshape(KEY_TILES, SUBLANES, LANES)
            return carry

        lax.fori_loop(0, PEER_HEADS, head, 0)

        acc_scr[...] = jnp.zeros_like(acc_scr)
        _activations(u_ref[...], h2_scr, s_a, slice(None), n_lg)

    def step(s_w, s_r):
        a0 = pl.multiple_of((e - 1) * na, na)
        for h in range(PEER_HEADS):
            cslab = c_scr[h, pl.ds(a0, na), :]
            eslab = e1_scr[h, pl.ds(a0, na), :]
            for r in range(na):
                cb = jnp.broadcast_to(cslab[r:r + 1, :], (SUBLANES, tt))
                eb = jnp.broadcast_to(eslab[r:r + 1, :], (SUBLANES, tt))
                for lg in range(n_lg):
                    cb_scr[h, r, lg] = cb[:, lg * LANES:(lg + 1) * LANES]
                    eb_scr[h, r, lg] = eb[:, lg * LANES:(lg + 1) * LANES]

        halves = [tuple(range(lo, min(lo + 2, n_lg))) for lo in range(0, n_lg, 2)]
        groups = [(half, ai, lg) for half in halves for ai in range(na) for lg in half]
        act_at = [ACT_LAG + p * (len(groups) // ACT_PIECES) for p in range(ACT_PIECES)]
        rows_per = na * PEER_NKEYS // ACT_PIECES
        ga = None
        for gi, (half, ai, lg) in enumerate(groups):
            if gi in act_at:
                p = act_at.index(gi)
                start = p * rows_per
                if ga is not None:
                    probe = jnp.max(ga[0:SUBLANES, :])
                    other = (start + rows_per) % (na * PEER_NKEYS)
                    start = jnp.where(probe != probe, other, start)
                start = pl.multiple_of(start, PACKED_ROWS)
                _activations(u_ref[pl.ds(start, rows_per), :], h2_scr, s_w,
                             pl.ds(start, rows_per), n_lg)
            gate = jnp.zeros((KEY_TILES, SUBLANES, LANES), F32)
            for h in range(PEER_HEADS):
                hit = s2_scr[h, lg] >= cb_scr[h, ai, lg][None]
                gate = gate + jnp.where(hit, e2_scr[h, lg], 0.0) * eb_scr[h, ai, lg][None]
            act = _gelu_tanh(s_r[lg, ai * PEER_NKEYS:(ai + 1) * PEER_NKEYS, :])
            ga = gate.reshape(PEER_NKEYS, LANES) * act
            ga_scr[ai * PEER_NKEYS:(ai + 1) * PEER_NKEYS, lg * LANES:(lg + 1) * LANES] = ga.astype(BF16)
            if ai == na - 1 and lg == half[-1]:
                hs = slice(half[0] * LANES, (half[-1] + 1) * LANES)
                acc_scr[:, hs] += jnp.dot(vt_ref[...], ga_scr[:, hs], preferred_element_type=F32)

    @pl.when((e > 0) & (e % 2 == 1))
    def _odd():
        step(s_b, s_a)

    @pl.when((e > 0) & (e % 2 == 0))
    def _even():
        step(s_a, s_b)

    @pl.when(e == n_chunks)
    def _fin():
        y = acc_scr[...].T
        o_ref[...] = _ln(ALPHA * x1_ref[...] + g2_ref[0] * y) * lng_ref[...] + lnb_ref[...]


def _peer(x1_flat, sc, sh, g2, wts, layer, tt, per_token, tiles_per_seq):
    t = x1_flat.shape[0]
    wqt_b, keys_b, u_b, vt_b, ln2_g, ln2_b = wts
    na = 8
    ec = na * PEER_NKEYS
    n_chunks = PEER_N // ec
    n_lg = tt // LANES
    mod = _mod_spec(per_token, tt, tiles_per_seq)
    return pl.pallas_call(
        functools.partial(_peer_kernel, tt=tt, na=na, n_chunks=n_chunks),
        grid=(t // tt, n_chunks + 1),
        in_specs=[
            pl.BlockSpec((tt, D_MODEL), lambda i, e: (i, 0)),
            mod, mod, mod,
            pl.BlockSpec((None, PEER_HEADS * PEER_DQ, D_MODEL), lambda i, e: (layer, 0, 0)),
            pl.BlockSpec((None, 2, PEER_NKEYS, PEER_DHALF), lambda i, e: (layer, 0, 0, 0)),
            pl.BlockSpec((None, ec, D_MODEL),
                         lambda i, e: (layer, jnp.minimum(e, n_chunks - 1), 0)),
            pl.BlockSpec((None, D_MODEL, ec), lambda i, e: (layer, 0, jnp.maximum(e - 1, 0))),
            pl.BlockSpec((None, 1, D_MODEL), lambda i, e: (layer, 0, 0)),
            pl.BlockSpec((None, 1, D_MODEL), lambda i, e: (layer, 0, 0)),
        ],
        out_specs=pl.BlockSpec((tt, D_MODEL), lambda i, e: (i, 0)),
        out_shape=jax.ShapeDtypeStruct((t, D_MODEL), F32),
        scratch_shapes=[
            pltpu.VMEM((tt, D_MODEL), BF16),
            pltpu.VMEM((PEER_HEADS * PEER_DQ, tt), BF16),
            pltpu.VMEM((PEER_HEADS, PEER_NKEYS, tt), F32),
            pltpu.VMEM((PEER_HEADS, PEER_NKEYS, tt), F32),
            pltpu.VMEM((PEER_HEADS, n_lg, KEY_TILES, SUBLANES, LANES), F32),
            pltpu.VMEM((PEER_HEADS, n_lg, KEY_TILES, SUBLANES, LANES), F32),
            pltpu.VMEM((PEER_HEADS, na, n_lg, SUBLANES, LANES), F32),
            pltpu.VMEM((PEER_HEADS, na, n_lg, SUBLANES, LANES), F32),
            pltpu.VMEM((n_lg, ec, LANES), F32),
            pltpu.VMEM((n_lg, ec, LANES), F32),
            pltpu.VMEM((ec, tt), BF16),
            pltpu.VMEM((D_MODEL, tt), F32),
        ],
        compiler_params=_params("arbitrary", "arbitrary"),
        name="peer",
    )(x1_flat, sc, sh, g2, wqt_b, keys_b, u_b, vt_b, ln2_g, ln2_b)


def _rope_tables(pos0, seq):
    half = DK_RET // 2
    inv = ROPE_BASE ** (-jnp.arange(half, dtype=F32) / half)
    ang = (pos0 + jnp.arange(seq)).astype(F32)[:, None] * inv[None, :]
    cos = jnp.cos(ang)
    sin = jnp.sin(ang)
    return jnp.concatenate([cos, cos], axis=-1), jnp.concatenate([-sin, sin], axis=-1)


def _trunk(x, mods, pool_hist, ret_state, pos0, wts, tiles):
    bsz, seq, _ = x.shape
    tt_mix, tt_peer = tiles
    per_token = seq < tt_peer
    cos2, sin2 = _rope_tables(pos0, seq)
    hist16 = jnp.pad(pool_hist, ((0, 0), (0, 0), (HIST_ROWS - POOL_HIST, 0), (0, 0)))
    new_hist, new_state = [], []
    for l in range(DEPTH):
        chunks = [mods[l, :, k * D_MODEL:(k + 1) * D_MODEL] for k in range(6)]
        seq_rows = [c[:, None, :] for c in chunks]
        if per_token:
            tok_rows = [jnp.repeat(c, seq, axis=0)[None] for c in chunks]
        else:
            tok_rows = seq_rows
        _, _, _, sh2, sc2, g2 = tok_rows
        sh1, sc1, g1 = seq_rows[:3]
        x1, s_l, hist_l = _mix(x, sc1, sh1, g1, cos2, sin2, hist16[l], ret_state[l],
                               (wts["w_in"], wts["w_grp"], wts["pool_scale"], wts["w_bp"],
                                wts["w_br"], wts["w_out"], wts["ln1_g"], wts["ln1_b"]),
                               l, tt_mix, pos0)
        x2 = _peer(x1.reshape(bsz * seq, D_MODEL), sc2, sh2, g2,
                   (wts["w_qt"], wts["keys"], wts["u"], wts["vt"], wts["ln2_g"], wts["ln2_b"]),
                   l, tt_peer, per_token, seq // tt_peer if not per_token else 1)
        x = x2.reshape(bsz, seq, D_MODEL)
        new_hist.append(hist_l[:, HIST_ROWS - POOL_HIST:, :])
        new_state.append(s_l)
    return x, jnp.stack(new_hist), jnp.stack(new_state)


def kernel(x_prompt, x_sample, cache_pool, state_ret, c_prompt, c_sample, w_ada, b_ada, w_in,
           w_pool_grp, pool_scale, w_branch_pool, w_branch_ret, w_out, ln1_g, ln1_b, w_peer_q,
           peer_sub_keys, peer_u, peer_v, ln2_g, ln2_b):
    n_prompt = x_prompt.shape[0]
    n_sample = x_sample.shape[0]
    wts = {
        "w_in": w_in.astype(BF16),
        "w_grp": w_pool_grp.astype(BF16),
        "pool_scale": pool_scale[:, None, :],
        "w_bp": w_branch_pool.astype(BF16),
        "w_br": w_branch_ret.astype(BF16),
        "w_out": w_out.astype(BF16),
        "ln1_g": ln1_g[:, None, :],
        "ln1_b": ln1_b[:, None, :],
        "w_qt": jnp.swapaxes(w_peer_q, 1, 2).astype(BF16),
        "keys": peer_sub_keys.astype(BF16),
        "u": peer_u.astype(BF16),
        "vt": jnp.swapaxes(peer_v, 1, 2).astype(BF16),
        "ln2_g": ln2_g[:, None, :],
        "ln2_b": ln2_b[:, None, :],
    }
    n_seq = n_prompt + n_sample
    c_pad = jnp.pad(jnp.concatenate([c_prompt, c_sample], axis=0), ((0, 16 - n_seq), (0, 0)))
    mods = _ada(c_pad, w_ada, b_ada)
    zero_hist = jnp.zeros((DEPTH, n_prompt, POOL_HIST, D_POOL), x_prompt.dtype)
    zero_state = jnp.zeros((DEPTH, n_prompt, N_RET_HEADS, DK_RET, DK_RET), F32)
    y_p, pool_p, ret_p = _trunk(x_prompt, mods[:, :n_prompt], zero_hist, zero_state, 0, wts,
                                (256, 512))
    y_s, pool_s, ret_s = _trunk(x_sample, mods[:, n_prompt:n_seq], cache_pool, state_ret,
                                PAST_LEN, wts, (32, 256))
    return (y_p, y_s, pool_p, ret_p.astype(state_ret.dtype), pool_s, ret_s.astype(state_ret.dtype))
```

```python
import functools
import math

import jax
import jax.numpy as jnp
from jax import lax
from jax.experimental import pallas as pl
from jax.experimental.pallas import tpu as pltpu

F32 = jnp.float32
BF16 = jnp.bfloat16

D_MODEL = 1024
DEPTH = 2
PAST_LEN = 2048
POOL_WINDOWS = (2, 4, 8, 16)
D_POOL = 512
POOL_GROUP = 128
POOL_HIST = 15
HIST_ROWS = 16
N_RET_HEADS = 8
DK_RET = 128
D_RET = N_RET_HEADS * DK_RET
RET_SCALE = DK_RET ** -0.5
ROPE_BASE = 10000.0
PEER_HEADS = 8
PEER_NKEYS = 128
PEER_N = PEER_NKEYS * PEER_NKEYS
PEER_DQ = 256
PEER_DHALF = 128
PEER_TOPK = 16
LN_EPS = 1e-5
ALPHA = (2 * DEPTH) ** 0.25
OFF_Q = D_POOL
OFF_K = OFF_Q + D_RET
OFF_V = OFF_K + D_RET
OFF_G = OFF_V + D_RET
OFF_BG = OFF_G + D_RET
D_IN = OFF_BG + 2 * D_MODEL
LOG_G = tuple(math.log(1.0 - 2.0 ** (-5.0 - h)) for h in range(N_RET_HEADS))
LANES = 128
SUBLANES = 8
KEY_TILES = PEER_NKEYS // SUBLANES
PACKED_ROWS = 16
ACT_PIECES = 2
ACT_LAG = 1
VMEM_LIMIT = 56 * 1024 * 1024
NEG_INF = float("-inf")


def _ln(x):
    mu = jnp.mean(x, axis=-1, keepdims=True)
    xc = x - mu
    var = jnp.mean(xc * xc, axis=-1, keepdims=True)
    return xc * lax.rsqrt(var + LN_EPS)


def _gelu_tanh(x):
    a0 = -2.0 * math.sqrt(2.0 / math.pi) * math.log2(math.e)
    return x / (1.0 + jnp.exp2(x * (a0 + (a0 * 0.044715) * (x * x))))


def _params(*sem):
    return pltpu.CompilerParams(dimension_semantics=sem, vmem_limit_bytes=VMEM_LIMIT)


def _ada_kernel(c_ref, w_ref, b_ref, o_ref):
    c = c_ref[...]
    a = (c * jax.nn.sigmoid(c)).astype(BF16)
    o_ref[...] = jnp.dot(a, w_ref[...].astype(BF16), preferred_element_type=F32) + b_ref[...]


def _ada(c_pad, w_ada, b_ada):
    rows = c_pad.shape[0]
    n_out = w_ada.shape[-1]
    ct = 1536
    return pl.pallas_call(
        _ada_kernel,
        grid=(DEPTH, n_out // ct),
        in_specs=[
            pl.BlockSpec((rows, D_MODEL), lambda l, j: (0, 0)),
            pl.BlockSpec((None, D_MODEL, ct), lambda l, j: (l, 0, j)),
            pl.BlockSpec((None, 1, ct), lambda l, j: (l, 0, j)),
        ],
        out_specs=pl.BlockSpec((None, rows, ct), lambda l, j: (l, 0, j)),
        out_shape=jax.ShapeDtypeStruct((DEPTH, rows, n_out), F32),
        compiler_params=_params("arbitrary", "arbitrary"),
        name="ada",
    )(c_pad, w_ada, b_ada.reshape(DEPTH, 1, n_out))


def _mod_spec(per_token, tt, tiles_per_seq):
    if per_token:
        return pl.BlockSpec((1, tt, D_MODEL), lambda i, *_: (0, i, 0))
    return pl.BlockSpec((1, 1, D_MODEL), lambda i, *_: (i // tiles_per_seq, 0, 0))


def _mix_kernel(x_ref, sc_ref, sh_ref, g1_ref, cos_ref, sin_ref, hist_ref, s0_ref, win_ref,
                wgrp_ref, pscale_ref, wbp_ref, wbr_ref, wout_ref, lng_ref, lnb_ref,
                x1_ref, sfin_ref, hout_ref,
                proj_ref, state_scr, ext_scr, decay_scr, xi_scr, zeta_scr, retg_scr, pool_scr,
                *, tt, pos0, n_tiles):
    j = pl.program_id(1)

    h_in = (_ln(x_ref[0]) * (1.0 + sc_ref[0]) + sh_ref[0]).astype(BF16)
    proj_ref[...] = jnp.dot(h_in, win_ref[...], preferred_element_type=F32)

    @pl.when(j == 0)
    def _init():
        state_scr[...] = s0_ref[0]
        ext_scr[0:HIST_ROWS, :] = hist_ref[0]
        row = lax.broadcasted_iota(jnp.int32, (tt, tt), 0)
        col = lax.broadcasted_iota(jnp.int32, (tt, tt), 1)
        causal = row >= col
        diff = jnp.where(causal, (row - col).astype(F32), 0.0)
        rowl = lax.broadcasted_iota(jnp.int32, (tt, LANES), 0).astype(F32)
        for h in range(N_RET_HEADS):
            decay_scr[h] = jnp.where(causal, jnp.exp(LOG_G[h] * diff), 0.0)
            xi_scr[h] = jnp.exp(LOG_G[h] * (rowl + 1.0))
            zeta_scr[h] = jnp.exp(LOG_G[h] * (tt - 1.0 - rowl))

    p = proj_ref[:,0:D_POOL]
    ext_scr[HIST_ROWS:HIST_ROWS + tt, :] = p
    pos = pos0 + j * tt + lax.broadcasted_iota(jnp.int32, (tt, POOL_GROUP), 0)
    for g, w in enumerate(POOL_WINDOWS):
        cs = slice(g * POOL_GROUP, (g + 1) * POOL_GROUP)
        ws = ext_scr[HIST_ROWS:HIST_ROWS + tt, cs]
        for d in range(1, w):
            ws = ws + ext_scr[HIST_ROWS - d:HIST_ROWS - d + tt, cs]
        cnt = jnp.minimum(pos + 1, w).astype(F32)
        pooled = ws / cnt - p[:, cs]
        mixed = jnp.dot(pooled.astype(BF16), wgrp_ref[g], preferred_element_type=F32)
        pool_scr[:, cs] = (mixed * pscale_ref[:, cs]).astype(BF16)
    ext_scr[0:HIST_ROWS, :] = ext_scr[tt:tt + HIST_ROWS, :]

    cosv = cos_ref[...]
    sinv = sin_ref[...]
    for h in range(N_RET_HEADS):
        hs = slice(h * DK_RET, (h + 1) * DK_RET)
        q = proj_ref[:,OFF_Q + h * DK_RET:OFF_Q + (h + 1) * DK_RET]
        k = proj_ref[:,OFF_K + h * DK_RET:OFF_K + (h + 1) * DK_RET]
        v = proj_ref[:,OFF_V + h * DK_RET:OFF_V + (h + 1) * DK_RET].astype(BF16)
        g = proj_ref[:,OFF_G + h * DK_RET:OFF_G + (h + 1) * DK_RET]
        qr = q * cosv + pltpu.roll(q, DK_RET // 2, 1) * sinv
        kr = (k * cosv + pltpu.roll(k, DK_RET // 2, 1) * sinv) * RET_SCALE
        qb = qr.astype(BF16)
        scores = lax.dot_general(qb, kr.astype(BF16), (((1,), (1,)), ((), ())),
                                 preferred_element_type=F32) * decay_scr[h]
        intra = jnp.dot(scores.astype(BF16), v, preferred_element_type=F32)
        s_prev = state_scr[h]
        cross = jnp.dot(qb, s_prev.astype(BF16), preferred_element_type=F32) * xi_scr[h]
        kz = (kr * zeta_scr[h]).astype(BF16)
        kv = lax.dot_general(kz, v, (((0,), (0,)), ((), ())), preferred_element_type=F32)
        state_scr[h] = math.exp(LOG_G[h] * tt) * s_prev + kv
        retg_scr[:, hs] = (_ln(intra + cross) * (g * jax.nn.sigmoid(g))).astype(BF16)

    bgp = jax.nn.sigmoid(proj_ref[:,OFF_BG:OFF_BG + D_MODEL])
    bgr = jax.nn.sigmoid(proj_ref[:,OFF_BG + D_MODEL:OFF_BG + 2 * D_MODEL])
    pb = jnp.dot(pool_scr[...], wbp_ref[...], preferred_element_type=F32)
    rb = jnp.dot(retg_scr[...], wbr_ref[...], preferred_element_type=F32)
    merged = (bgp * pb + bgr * rb).astype(BF16)
    z = jnp.dot(merged, wout_ref[...], preferred_element_type=F32)
    x1_ref[0] = _ln(ALPHA * x_ref[0] + g1_ref[0] * z) * lng_ref[...] + lnb_ref[...]

    @pl.when(j == n_tiles - 1)
    def _fin():
        sfin_ref[0] = state_scr[...]
        hout_ref[0] = ext_scr[0:HIST_ROWS, :]


def _mix(x, sc1, sh1, g1, cos2, sin2, hist16, s0, wts, layer, tt, pos0):
    bsz, seq, _ = x.shape
    n_tiles = seq // tt
    w_in_b, wgrp_b, pscale, wbp_b, wbr_b, wout_b, ln1_g, ln1_b = wts
    const2 = lambda b, j: (layer, 0, 0)
    once = pl.Buffered(1)
    seq_row = pl.BlockSpec((1, 1, D_MODEL), lambda b, j: (b, 0, 0))
    return pl.pallas_call(
        functools.partial(_mix_kernel, tt=tt, pos0=pos0, n_tiles=n_tiles),
        grid=(bsz, n_tiles),
        in_specs=[
            pl.BlockSpec((1, tt, D_MODEL), lambda b, j: (b, j, 0)),
            seq_row, seq_row, seq_row,
            pl.BlockSpec((tt, LANES), lambda b, j: (j, 0)),
            pl.BlockSpec((tt, LANES), lambda b, j: (j, 0)),
            pl.BlockSpec((1, HIST_ROWS, D_POOL), lambda b, j: (b, 0, 0)),
            pl.BlockSpec((1, N_RET_HEADS, DK_RET, DK_RET), lambda b, j: (b, 0, 0, 0)),
            pl.BlockSpec((None, D_MODEL, D_IN), const2, pipeline_mode=once),
            pl.BlockSpec((None, len(POOL_WINDOWS), POOL_GROUP, POOL_GROUP),
                         lambda b, j: (layer, 0, 0, 0)),
            pl.BlockSpec((None, 1, D_POOL), const2),
            pl.BlockSpec((None, D_POOL, D_MODEL), const2, pipeline_mode=once),
            pl.BlockSpec((None, D_RET, D_MODEL), const2, pipeline_mode=once),
            pl.BlockSpec((None, D_MODEL, D_MODEL), const2, pipeline_mode=once),
            pl.BlockSpec((None, 1, D_MODEL), const2),
            pl.BlockSpec((None, 1, D_MODEL), const2),
        ],
        out_specs=[
            pl.BlockSpec((1, tt, D_MODEL), lambda b, j: (b, j, 0)),
            pl.BlockSpec((1, N_RET_HEADS, DK_RET, DK_RET), lambda b, j: (b, 0, 0, 0)),
            pl.BlockSpec((1, HIST_ROWS, D_POOL), lambda b, j: (b, 0, 0)),
        ],
        out_shape=[
            jax.ShapeDtypeStruct((bsz, seq, D_MODEL), F32),
            jax.ShapeDtypeStruct((bsz, N_RET_HEADS, DK_RET, DK_RET), F32),
            jax.ShapeDtypeStruct((bsz, HIST_ROWS, D_POOL), F32),
        ],
        scratch_shapes=[
            pltpu.VMEM((tt, D_IN), F32),
            pltpu.VMEM((N_RET_HEADS, DK_RET, DK_RET), F32),
            pltpu.VMEM((HIST_ROWS + tt, D_POOL), F32),
            pltpu.VMEM((N_RET_HEADS, tt, tt), F32),
            pltpu.VMEM((N_RET_HEADS, tt, LANES), F32),
            pltpu.VMEM((N_RET_HEADS, tt, LANES), F32),
            pltpu.VMEM((tt, D_RET), BF16),
            pltpu.VMEM((tt, D_POOL), BF16),
        ],
        compiler_params=_params("arbitrary", "arbitrary"),
        name="mix",
    )(x, sc1, sh1, g1, cos2, sin2, hist16, s0, w_in_b, wgrp_b, pscale, wbp_b, wbr_b, wout_b,
      ln1_g, ln1_b)


def _sort_tiles_desc(tiles):
    a = list(tiles)
    n = len(a)
    k = 2
    while k <= n:
        j = k // 2
        while j >= 1:
            for i in range(n):
                l = i ^ j
                if l > i:
                    hi = jnp.maximum(a[i], a[l])
                    lo = jnp.minimum(a[i], a[l])
                    a[i], a[l] = (hi, lo) if (i & k) == 0 else (lo, hi)
            j //= 2
        k *= 2
    return a


def _merge_top16(lists, extra=None):
    cur = list(lists)
    rows = []
    for r in range(PEER_TOPK):
        m = jnp.max(cur[0], axis=0, keepdims=True)
        if extra is not None:
            m = jnp.maximum(m, jnp.max(extra, axis=0, keepdims=True))
        rows.append(m)
        if r + 1 < PEER_TOPK:
            hit = cur[0] == m
            cur = [jnp.where(hit, cur[j + 1], cur[j]) for j in range(PEER_TOPK - 1 - r)]
            if extra is not None:
                extra = jnp.where(extra == m, NEG_INF, extra)
    return rows


def _top16_rows(s):
    tiles = [s[k * SUBLANES:(k + 1) * SUBLANES, :] for k in range(KEY_TILES)]
    return _merge_top16(_sort_tiles_desc(tiles))


def _stack_rows(rows, tt):
    ridx = lax.broadcasted_iota(jnp.int32, (len(rows), tt), 0)
    out = jnp.zeros((len(rows), tt), F32)
    for i, r in enumerate(rows):
        out = jnp.where(ridx == i, r, out)
    return out


def _activations(u_rows, h2_scr, s_dst, rows, n_lg):
    s_val = lax.dot_general(u_rows, h2_scr[...], (((1,), (1,)), ((), ())),
                            preferred_element_type=F32)
    for lg in range(n_lg):
        s_dst[lg, rows, :] = s_val[:, lg * LANES:(lg + 1) * LANES]


def _peer_kernel(x1_ref, sc_ref, sh_ref, g2_ref, wqt_ref, keys_ref, u_ref, vt_ref,
                 lng_ref, lnb_ref, o_ref,
                 h2_scr, qt_scr, c_scr, e1_scr, s2_scr, e2_scr, cb_scr, eb_scr, s_a, s_b, ga_scr, acc_scr,
                 *, tt, na, n_chunks):
    e = pl.program_id(1)
    n_lg = tt // LANES

    @pl.when(e == 0)
    def _prep():
        h2 = (_ln(x1_ref[...]) * (1.0 + sc_ref[0]) + sh_ref[0]).astype(BF16)
        h2_scr[...] = h2
        qt = lax.dot_general(wqt_ref[...], h2, (((1,), (1,)), ((), ())),
                             preferred_element_type=F32)
        qt_scr[...] = qt.astype(BF16)

        def head(h, carry):
            base = pl.multiple_of(h * PEER_DQ, PEER_DQ)
            s1 = jnp.dot(keys_ref[0], qt_scr[pl.ds(base, PEER_DHALF), :],
                         preferred_element_type=F32)
            s2 = jnp.dot(keys_ref[1], qt_scr[pl.ds(base + PEER_DHALF, PEER_DHALF), :],
                         preferred_element_type=F32)
            r1 = _top16_rows(s1)
            r2 = _top16_rows(s2)
            v2_lo = _stack_rows(r2[:SUBLANES], tt)
            v2_hi = _stack_rows(r2[SUBLANES:], tt)
            cand = [r1[i] + v2_lo for i in range(PEER_TOPK)]
            singles = r1[0] + v2_hi
            tau = _merge_top16(cand, singles)[PEER_TOPK - 1]
            top = r1[0] + r2[0]
            zsum = jnp.zeros((SUBLANES, tt), F32)
            for c in cand + [singles]:
                zsum = zsum + jnp.where(c >= tau, jnp.exp(c - top), 0.0)
            z = jnp.sum(zsum, axis=0, keepdims=True)
            c_scr[h] = tau - s1
            e1_scr[h] = jnp.exp(s1 - r1[0])
            e2 = jnp.exp(s2 - r2[0]) / z
            for lg in range(n_lg):
                ls = slice(lg * LANES, (lg + 1) * LANES)
                s2_scr[h, lg] = s2[:, ls].reshape(KEY_TILES, SUBLANES, LANES)
                e2_scr[h, lg] = e2[:, ls].reshape(KEY_TILES, SUBLANES, LANES)
            return carry

        lax.fori_loop(0, PEER_HEADS, head, 0)

        acc_scr[...] = jnp.zeros_like(acc_scr)
        _activations(u_ref[...], h2_scr, s_a, slice(None), n_lg)

    def step(s_w, s_r):
        a0 = pl.multiple_of((e - 1) * na, na)
        for h in range(PEER_HEADS):
            cslab = c_scr[h, pl.ds(a0, na), :]
            eslab = e1_scr[h, pl.ds(a0, na), :]
            for r in range(na):
                cb = jnp.broadcast_to(cslab[r:r + 1, :], (SUBLANES, tt))
                eb = jnp.broadcast_to(eslab[r:r + 1, :], (SUBLANES, tt))
                for lg in range(n_lg):
                    cb_scr[h, r, lg] = cb[:, lg * LANES:(lg + 1) * LANES]
                    eb_scr[h, r, lg] = eb[:, lg * LANES:(lg + 1) * LANES]

        halves = [tuple(range(lo, min(lo + 2, n_lg))) for lo in range(0, n_lg, 2)]
        groups = [(half, ai, lg) for half in halves for ai in range(na) for lg in half]
        act_at = [ACT_LAG + p * (len(groups) // ACT_PIECES) for p in range(ACT_PIECES)]
        rows_per = na * PEER_NKEYS // ACT_PIECES
        ga = None
        for gi, (half, ai, lg) in enumerate(groups):
            if gi in act_at:
                p = act_at.index(gi)
                start = p * rows_per
                if ga is not None:
                    probe = jnp.max(ga[0:SUBLANES, :])
                    other = (start + rows_per) % (na * PEER_NKEYS)
                    start = jnp.where(probe != probe, other, start)
                start = pl.multiple_of(start, PACKED_ROWS)
                _activations(u_ref[pl.ds(start, rows_per), :], h2_scr, s_w,
                             pl.ds(start, rows_per), n_lg)
            gate = jnp.zeros((KEY_TILES, SUBLANES, LANES), F32)
            for h in range(PEER_HEADS):
                hit = s2_scr[h, lg] >= cb_scr[h, ai, lg][None]
                gate = gate + jnp.where(hit, e2_scr[h, lg], 0.0) * eb_scr[h, ai, lg][None]
            act = _gelu_tanh(s_r[lg, ai * PEER_NKEYS:(ai + 1) * PEER_NKEYS, :])
            ga = gate.reshape(PEER_NKEYS, LANES) * act
            ga_scr[ai * PEER_NKEYS:(ai + 1) * PEER_NKEYS, lg * LANES:(lg + 1) * LANES] = ga.astype(BF16)
            if ai == na - 1 and lg == half[-1]:
                hs = slice(half[0] * LANES, (half[-1] + 1) * LANES)
                acc_scr[:, hs] += jnp.dot(vt_ref[...], ga_scr[:, hs], preferred_element_type=F32)

    @pl.when((e > 0) & (e % 2 == 1))
    def _odd():
        step(s_b, s_a)

    @pl.when((e > 0) & (e % 2 == 0))
    def _even():
        step(s_a, s_b)

    @pl.when(e == n_chunks)
    def _fin():
        y = acc_scr[...].T
        o_ref[...] = _ln(ALPHA * x1_ref[...] + g2_ref[0] * y) * lng_ref[...] + lnb_ref[...]


def _peer(x1_flat, sc, sh, g2, wts, layer, tt, per_token, tiles_per_seq):
    t = x1_flat.shape[0]
    wqt_b, keys_b, u_b, vt_b, ln2_g, ln2_b = wts
    na = 8
    ec = na * PEER_NKEYS
    n_chunks = PEER_N // ec
    n_lg = tt // LANES
    mod = _mod_spec(per_token, tt, tiles_per_seq)
    return pl.pallas_call(
        functools.partial(_peer_kernel, tt=tt, na=na, n_chunks=n_chunks),
        grid=(t // tt, n_chunks + 1),
        in_specs=[
            pl.BlockSpec((tt, D_MODEL), lambda i, e: (i, 0)),
            mod, mod, mod,
            pl.BlockSpec((None, PEER_HEADS * PEER_DQ, D_MODEL), lambda i, e: (layer, 0, 0)),
            pl.BlockSpec((None, 2, PEER_NKEYS, PEER_DHALF), lambda i, e: (layer, 0, 0, 0)),
            pl.BlockSpec((None, ec, D_MODEL),
                         lambda i, e: (layer, jnp.minimum(e, n_chunks - 1), 0)),
            pl.BlockSpec((None, D_MODEL, ec), lambda i, e: (layer, 0, jnp.maximum(e - 1, 0))),
            pl.BlockSpec((None, 1, D_MODEL), lambda i, e: (layer, 0, 0)),
            pl.BlockSpec((None, 1, D_MODEL), lambda i, e: (layer, 0, 0)),
        ],
        out_specs=pl.BlockSpec((tt, D_MODEL), lambda i, e: (i, 0)),
        out_shape=jax.ShapeDtypeStruct((t, D_MODEL), F32),
        scratch_shapes=[
            pltpu.VMEM((tt, D_MODEL), BF16),
            pltpu.VMEM((PEER_HEADS * PEER_DQ, tt), BF16),
            pltpu.VMEM((PEER_HEADS, PEER_NKEYS, tt), F32),
            pltpu.VMEM((PEER_HEADS, PEER_NKEYS, tt), F32),
            pltpu.VMEM((PEER_HEADS, n_lg, KEY_TILES, SUBLANES, LANES), F32),
            pltpu.VMEM((PEER_HEADS, n_lg, KEY_TILES, SUBLANES, LANES), F32),
            pltpu.VMEM((PEER_HEADS, na, n_lg, SUBLANES, LANES), F32),
            pltpu.VMEM((PEER_HEADS, na, n_lg, SUBLANES, LANES), F32),
            pltpu.VMEM((n_lg, ec, LANES), F32),
            pltpu.VMEM((n_lg, ec, LANES), F32),
            pltpu.VMEM((ec, tt), BF16),
            pltpu.VMEM((D_MODEL, tt), F32),
        ],
        compiler_params=_params("arbitrary", "arbitrary"),
        name="peer",
    )(x1_flat, sc, sh, g2, wqt_b, keys_b, u_b, vt_b, ln2_g, ln2_b)


def _rope_tables(pos0, seq):
    half = DK_RET // 2
    inv = ROPE_BASE ** (-jnp.arange(half, dtype=F32) / half)
    ang = (pos0 + jnp.arange(seq)).astype(F32)[:, None] * inv[None, :]
    cos = jnp.cos(ang)
    sin = jnp.sin(ang)
    return jnp.concatenate([cos, cos], axis=-1), jnp.concatenate([-sin, sin], axis=-1)


def _trunk(x, mods, pool_hist, ret_state, pos0, wts, tiles):
    bsz, seq, _ = x.shape
    tt_mix, tt_peer = tiles
    per_token = seq < tt_peer
    cos2, sin2 = _rope_tables(pos0, seq)
    hist16 = jnp.pad(pool_hist, ((0, 0), (0, 0), (HIST_ROWS - POOL_HIST, 0), (0, 0)))
    new_hist, new_state = [], []
    for l in range(DEPTH):
        chunks = [mods[l, :, k * D_MODEL:(k + 1) * D_MODEL] for k in range(6)]
        seq_rows = [c[:, None, :] for c in chunks]
        if per_token:
            tok_rows = [jnp.repeat(c, seq, axis=0)[None] for c in chunks]
        else:
            tok_rows = seq_rows
        _, _, _, sh2, sc2, g2 = tok_rows
        sh1, sc1, g1 = seq_rows[:3]
        x1, s_l, hist_l = _mix(x, sc1, sh1, g1, cos2, sin2, hist16[l], ret_state[l],
                               (wts["w_in"], wts["w_grp"], wts["pool_scale"], wts["w_bp"],
                                wts["w_br"], wts["w_out"], wts["ln1_g"], wts["ln1_b"]),
                               l, tt_mix, pos0)
        x2 = _peer(x1.reshape(bsz * seq, D_MODEL), sc2, sh2, g2,
                   (wts["w_qt"], wts["keys"], wts["u"], wts["vt"], wts["ln2_g"], wts["ln2_b"]),
                   l, tt_peer, per_token, seq // tt_peer if not per_token else 1)
        x = x2.reshape(bsz, seq, D_MODEL)
        new_hist.append(hist_l[:, HIST_ROWS - POOL_HIST:, :])
        new_state.append(s_l)
    return x, jnp.stack(new_hist), jnp.stack(new_state)


def kernel(x_prompt, x_sample, cache_pool, state_ret, c_prompt, c_sample, w_ada, b_ada, w_in,
           w_pool_grp, pool_scale, w_branch_pool, w_branch_ret, w_out, ln1_g, ln1_b, w_peer_q,
           peer_sub_keys, peer_u, peer_v, ln2_g, ln2_b):
    n_prompt = x_prompt.shape[0]
    n_sample = x_sample.shape[0]
    wts = {
        "w_in": w_in.astype(BF16),
        "w_grp": w_pool_grp.astype(BF16),
        "pool_scale": pool_scale[:, None, :],
        "w_bp": w_branch_pool.astype(BF16),
        "w_br": w_branch_ret.astype(BF16),
        "w_out": w_out.astype(BF16),
        "ln1_g": ln1_g[:, None, :],
        "ln1_b": ln1_b[:, None, :],
        "w_qt": jnp.swapaxes(w_peer_q, 1, 2).astype(BF16),
        "keys": peer_sub_keys.astype(BF16),
        "u": peer_u.astype(BF16),
        "vt": jnp.swapaxes(peer_v, 1, 2).astype(BF16),
        "ln2_g": ln2_g[:, None, :],
        "ln2_b": ln2_b[:, None, :],
    }
    n_seq = n_prompt + n_sample
    c_pad = jnp.pad(jnp.concatenate([c_prompt, c_sample], axis=0), ((0, 16 - n_seq), (0, 0)))
    mods = _ada(c_pad, w_ada, b_ada)
    zero_hist = jnp.zeros((DEPTH, n_prompt, POOL_HIST, D_POOL), x_prompt.dtype)
    zero_state = jnp.zeros((DEPTH, n_prompt, N_RET_HEADS, DK_RET, DK_RET), F32)
    y_p, pool_p, ret_p = _trunk(x_prompt, mods[:, :n_prompt], zero_hist, zero_state, 0, wts,
                                (256, 512))
    y_s, pool_s, ret_s = _trunk(x_sample, mods[:, n_prompt:n_seq], cache_pool, state_ret,
                                PAST_LEN, wts, (32, 256))
    return (y_p, y_s, pool_p, ret_p.astype(state_ret.dtype), pool_s, ret_s.astype(state_ret.dtype))
```

```python
import functools
import math

import jax
import jax.numpy as jnp
from jax import lax
from jax.experimental import pallas as pl
from jax.experimental.pallas import tpu as pltpu

F32 = jnp.float32
BF16 = jnp.bfloat16

D_MODEL = 1024
DEPTH = 2
PAST_LEN = 2048
POOL_WINDOWS = (2, 4, 8, 16)
D_POOL = 512
POOL_GROUP = 128
POOL_HIST = 15
HIST_ROWS = 16
N_RET_HEADS = 8
DK_RET = 128
D_RET = N_RET_HEADS * DK_RET
RET_SCALE = DK_RET ** -0.5
ROPE_BASE = 10000.0
PEER_HEADS = 8
PEER_NKEYS = 128
PEER_N = PEER_NKEYS * PEER_NKEYS
PEER_DQ = 256
PEER_DHALF = 128
PEER_TOPK = 16
LN_EPS = 1e-5
ALPHA = (2 * DEPTH) ** 0.25
OFF_Q = D_POOL
OFF_K = OFF_Q + D_RET
OFF_V = OFF_K + D_RET
OFF_G = OFF_V + D_RET
OFF_BG = OFF_G + D_RET
D_IN = OFF_BG + 2 * D_MODEL
LOG_G = tuple(math.log(1.0 - 2.0 ** (-5.0 - h)) for h in range(N_RET_HEADS))
LANES = 128
SUBLANES = 8
KEY_TILES = PEER_NKEYS // SUBLANES
PACKED_ROWS = 16
ACT_PIECES = 2
ACT_LAG = 1
VMEM_LIMIT = 56 * 1024 * 1024
NEG_INF = float("-inf")


def _ln(x):
    mu = jnp.mean(x, axis=-1, keepdims=True)
    xc = x - mu
    var = jnp.mean(xc * xc, axis=-1, keepdims=True)
    return xc * lax.rsqrt(var + LN_EPS)


def _gelu_tanh(x):
    a0 = -2.0 * math.sqrt(2.0 / math.pi) * math.log2(math.e)
    return x / (1.0 + jnp.exp2(x * (a0 + (a0 * 0.044715) * (x * x))))


def _params(*sem):
    return pltpu.CompilerParams(dimension_semantics=sem, vmem_limit_bytes=VMEM_LIMIT)


def _ada_kernel(c_ref, w_ref, b_ref, o_ref):
    c = c_ref[...]
    a = (c * jax.nn.sigmoid(c)).astype(BF16)
    o_ref[...] = jnp.dot(a, w_ref[...].astype(BF16), preferred_element_type=F32) + b_ref[...]


def _ada(c_pad, w_ada, b_ada):
    rows = c_pad.shape[0]
    n_out = w_ada.shape[-1]
    ct = 1536
    return pl.pallas_call(
        _ada_kernel,
        grid=(DEPTH, n_out // ct),
        in_specs=[
            pl.BlockSpec((rows, D_MODEL), lambda l, j: (0, 0)),
            pl.BlockSpec((None, D_MODEL, ct), lambda l, j: (l, 0, j)),
            pl.BlockSpec((None, 1, ct), lambda l, j: (l, 0, j)),
        ],
        out_specs=pl.BlockSpec((None, rows, ct), lambda l, j: (l, 0, j)),
        out_shape=jax.ShapeDtypeStruct((DEPTH, rows, n_out), F32),
        compiler_params=_params("arbitrary", "arbitrary"),
        name="ada",
    )(c_pad, w_ada, b_ada.reshape(DEPTH, 1, n_out))


def _mod_spec(per_token, tt, tiles_per_seq):
    if per_token:
        return pl.BlockSpec((1, tt, D_MODEL), lambda i, *_: (0, i, 0))
    return pl.BlockSpec((1, 1, D_MODEL), lambda i, *_: (i // tiles_per_seq, 0, 0))


def _mix_kernel(x_ref, sc_ref, sh_ref, g1_ref, cos_ref, sin_ref, hist_ref, s0_ref, win_ref,
                wgrp_ref, pscale_ref, wbp_ref, wbr_ref, wout_ref, lng_ref, lnb_ref,
                x1_ref, sfin_ref, hout_ref,
                proj_ref, state_scr, ext_scr, decay_scr, xi_scr, zeta_scr, retg_scr, pool_scr,
                *, tt, pos0, n_tiles):
    j = pl.program_id(1)

    h_in = (_ln(x_ref[0]) * (1.0 + sc_ref[0]) + sh_ref[0]).astype(BF16)
    proj_ref[...] = jnp.dot(h_in, win_ref[...], preferred_element_type=F32)

    @pl.when(j == 0)
    def _init():
        state_scr[...] = s0_ref[0]
        ext_scr[0:HIST_ROWS, :] = hist_ref[0]
        row = lax.broadcasted_iota(jnp.int32, (tt, tt), 0)
        col = lax.broadcasted_iota(jnp.int32, (tt, tt), 1)
        causal = row >= col
        diff = jnp.where(causal, (row - col).astype(F32), 0.0)
        rowl = lax.broadcasted_iota(jnp.int32, (tt, LANES), 0).astype(F32)
        for h in range(N_RET_HEADS):
            decay_scr[h] = jnp.where(causal, jnp.exp(LOG_G[h] * diff), 0.0)
            xi_scr[h] = jnp.exp(LOG_G[h] * (rowl + 1.0))
            zeta_scr[h] = jnp.exp(LOG_G[h] * (tt - 1.0 - rowl))

    p = proj_ref[:,0:D_POOL]
    ext_scr[HIST_ROWS:HIST_ROWS + tt, :] = p
    pos = pos0 + j * tt + lax.broadcasted_iota(jnp.int32, (tt, POOL_GROUP), 0)
    for g, w in enumerate(POOL_WINDOWS):
        cs = slice(g * POOL_GROUP, (g + 1) * POOL_GROUP)
        ws = ext_scr[HIST_ROWS:HIST_ROWS + tt, cs]
        for d in range(1, w):
            ws = ws + ext_scr[HIST_ROWS - d:HIST_ROWS - d + tt, cs]
        cnt = jnp.minimum(pos + 1, w).astype(F32)
        pooled = ws / cnt - p[:, cs]
        mixed = jnp.dot(pooled.astype(BF16), wgrp_ref[g], preferred_element_type=F32)
        pool_scr[:, cs] = (mixed * pscale_ref[:, cs]).astype(BF16)
    ext_scr[0:HIST_ROWS, :] = ext_scr[tt:tt + HIST_ROWS, :]

    cosv = cos_ref[...]
    sinv = sin_ref[...]
    for h in range(N_RET_HEADS):
        hs = slice(h * DK_RET, (h + 1) * DK_RET)
        q = proj_ref[:,OFF_Q + h * DK_RET:OFF_Q + (h + 1) * DK_RET]
        k = proj_ref[:,OFF_K + h * DK_RET:OFF_K + (h + 1) * DK_RET]
        v = proj_ref[:,OFF_V + h * DK_RET:OFF_V + (h + 1) * DK_RET].astype(BF16)
        g = proj_ref[:,OFF_G + h * DK_RET:OFF_G + (h + 1) * DK_RET]
        qr = q * cosv + pltpu.roll(q, DK_RET // 2, 1) * sinv
        kr = (k * cosv + pltpu.roll(k, DK_RET // 2, 1) * sinv) * RET_SCALE
        qb = qr.astype(BF16)
        scores = lax.dot_general(qb, kr.astype(BF16), (((1,), (1,)), ((), ())),
                                 preferred_element_type=F32) * decay_scr[h]
        intra = jnp.dot(scores.astype(BF16), v, preferred_element_type=F32)
        s_prev = state_scr[h]
        cross = jnp.dot(qb, s_prev.astype(BF16), preferred_element_type=F32) * xi_scr[h]
        kz = (kr * zeta_scr[h]).astype(BF16)
        kv = lax.dot_general(kz, v, (((0,), (0,)), ((), ())), preferred_element_type=F32)
        state_scr[h] = math.exp(LOG_G[h] * tt) * s_prev + kv
        retg_scr[:, hs] = (_ln(intra + cross) * (g * jax.nn.sigmoid(g))).astype(BF16)

    bgp = jax.nn.sigmoid(proj_ref[:,OFF_BG:OFF_BG + D_MODEL])
    bgr = jax.nn.sigmoid(proj_ref[:,OFF_BG + D_MODEL:OFF_BG + 2 * D_MODEL])
    pb = jnp.dot(pool_scr[...], wbp_ref[...], preferred_element_type=F32)
    rb = jnp.dot(retg_scr[...], wbr_ref[...], preferred_element_type=F32)
    merged = (bgp * pb + bgr * rb).astype(BF16)
    z = jnp.dot(merged, wout_ref[...], preferred_element_type=F32)
    x1_ref[0] = _ln(ALPHA * x_ref[0] + g1_ref[0] * z) * lng_ref[...] + lnb_ref[...]

    @pl.when(j == n_tiles - 1)
    def _fin():
        sfin_ref[0] = state_scr[...]
        hout_ref[0] = ext_scr[0:HIST_ROWS, :]


def _mix(x, sc1, sh1, g1, cos2, sin2, hist16, s0, wts, layer, tt, pos0):
    bsz, seq, _ = x.shape
    n_tiles = seq // tt
    w_in_b, wgrp_b, pscale, wbp_b, wbr_b, wout_b, ln1_g, ln1_b = wts
    const2 = lambda b, j: (layer, 0, 0)
    once = pl.Buffered(1)
    seq_row = pl.BlockSpec((1, 1, D_MODEL), lambda b, j: (b, 0, 0))
    return pl.pallas_call(
        functools.partial(_mix_kernel, tt=tt, pos0=pos0, n_tiles=n_tiles),
        grid=(bsz, n_tiles),
        in_specs=[
            pl.BlockSpec((1, tt, D_MODEL), lambda b, j: (b, j, 0)),
            seq_row, seq_row, seq_row,
            pl.BlockSpec((tt, LANES), lambda b, j: (j, 0)),
            pl.BlockSpec((tt, LANES), lambda b, j: (j, 0)),
            pl.BlockSpec((1, HIST_ROWS, D_POOL), lambda b, j: (b, 0, 0)),
            pl.BlockSpec((1, N_RET_HEADS, DK_RET, DK_RET), lambda b, j: (b, 0, 0, 0)),
            pl.BlockSpec((None, D_MODEL, D_IN), const2, pipeline_mode=once),
            pl.BlockSpec((None, len(POOL_WINDOWS), POOL_GROUP, POOL_GROUP),
                         lambda b, j: (layer, 0, 0, 0)),
            pl.BlockSpec((None, 1, D_POOL), const2),
            pl.BlockSpec((None, D_POOL, D_MODEL), const2, pipeline_mode=once),
            pl.BlockSpec((None, D_RET, D_MODEL), const2, pipeline_mode=once),
            pl.BlockSpec((None, D_MODEL, D_MODEL), const2, pipeline_mode=once),
            pl.BlockSpec((None, 1, D_MODEL), const2),
            pl.BlockSpec((None, 1, D_MODEL), const2),
        ],
        out_specs=[
            pl.BlockSpec((1, tt, D_MODEL), lambda b, j: (b, j, 0)),
            pl.BlockSpec((1, N_RET_HEADS, DK_RET, DK_RET), lambda b, j: (b, 0, 0, 0)),
            pl.BlockSpec((1, HIST_ROWS, D_POOL), lambda b, j: (b, 0, 0)),
        ],
        out_shape=[
            jax.ShapeDtypeStruct((bsz, seq, D_MODEL), F32),
            jax.ShapeDtypeStruct((bsz, N_RET_HEADS, DK_RET, DK_RET), F32),
            jax.ShapeDtypeStruct((bsz, HIST_ROWS, D_POOL), F32),
        ],
        scratch_shapes=[
            pltpu.VMEM((tt, D_IN), F32),
            pltpu.VMEM((N_RET_HEADS, DK_RET, DK_RET), F32),
            pltpu.VMEM((HIST_ROWS + tt, D_POOL), F32),
            pltpu.VMEM((N_RET_HEADS, tt, tt), F32),
            pltpu.VMEM((N_RET_HEADS, tt, LANES), F32),
            pltpu.VMEM((N_RET_HEADS, tt, LANES), F32),
            pltpu.VMEM((tt, D_RET), BF16),
            pltpu.VMEM((tt, D_POOL), BF16),
        ],
        compiler_params=_params("arbitrary", "arbitrary"),
        name="mix",
    )(x, sc1, sh1, g1, cos2, sin2, hist16, s0, w_in_b, wgrp_b, pscale, wbp_b, wbr_b, wout_b,
      ln1_g, ln1_b)


def _sort_tiles_desc(tiles):
    a = list(tiles)
    n = len(a)
    k = 2
    while k <= n:
        j = k // 2
        while j >= 1:
            for i in range(n):
                l = i ^ j
                if l > i:
                    hi = jnp.maximum(a[i], a[l])
                    lo = jnp.minimum(a[i], a[l])
                    a[i], a[l] = (hi, lo) if (i & k) == 0 else (lo, hi)
            j //= 2
        k *= 2
    return a


def _merge_top16(lists, extra=None):
    cur = list(lists)
    rows = []
    for r in range(PEER_TOPK):
        m = jnp.max(cur[0], axis=0, keepdims=True)
        if extra is not None:
            m = jnp.maximum(m, jnp.max(extra, axis=0, keepdims=True))
        rows.append(m)
        if r + 1 < PEER_TOPK:
            hit = cur[0] == m
            cur = [jnp.where(hit, cur[j + 1], cur[j]) for j in range(PEER_TOPK - 1 - r)]
            if extra is not None:
                extra = jnp.where(extra == m, NEG_INF, extra)
    return rows


def _top16_rows(s):
    tiles = [s[k * SUBLANES:(k + 1) * SUBLANES, :] for k in range(KEY_TILES)]
    return _merge_top16(_sort_tiles_desc(tiles))


def _stack_rows(rows, tt):
    ridx = lax.broadcasted_iota(jnp.int32, (len(rows), tt), 0)
    out = jnp.zeros((len(rows), tt), F32)
    for i, r in enumerate(rows):
        out = jnp.where(ridx == i, r, out)
    return out


def _activations(u_rows, h2_scr, s_dst, rows, n_lg):
    s_val = jnp.dot(u_rows, h2_scr[...], preferred_element_type=F32)
    for lg in range(n_lg):
        s_dst[lg, rows, :] = s_val[:, lg * LANES:(lg + 1) * LANES]


def _peer_kernel(x1_ref, sc_ref, sh_ref, g2_ref, wqt_ref, keys_ref, u_ref, vt_ref,
                 lng_ref, lnb_ref, o_ref,
                 h2_scr, qt_scr, c_scr, e1_scr, s2_scr, e2_scr, cb_scr, eb_scr, s_a, s_b, ga_scr, acc_scr,
                 *, tt, na, n_chunks):
    e = pl.program_id(1)
    n_lg = tt // LANES

    @pl.when(e == 0)
    def _prep():
        h2 = _ln(x1_ref[...]) * (1.0 + sc_ref[0]) + sh_ref[0]
        h2_scr[...] = h2.T.astype(BF16)
        qt = jnp.dot(wqt_ref[...], h2_scr[...], preferred_element_type=F32)
        qt_scr[...] = qt.astype(BF16)

        def head(h, carry):
            base = pl.multiple_of(h * PEER_DQ, PEER_DQ)
            s1 = jnp.dot(keys_ref[0], qt_scr[pl.ds(base, PEER_DHALF), :],
                         preferred_element_type=F32)
            s2 = jnp.dot(keys_ref[1], qt_scr[pl.ds(base + PEER_DHALF, PEER_DHALF), :],
                         preferred_element_type=F32)
            r1 = _top16_rows(s1)
            r2 = _top16_rows(s2)
            v2_lo = _stack_rows(r2[:SUBLANES], tt)
            v2_hi = _stack_rows(r2[SUBLANES:], tt)
            cand = [r1[i] + v2_lo for i in range(PEER_TOPK)]
            singles = r1[0] + v2_hi
            tau = _merge_top16(cand, singles)[PEER_TOPK - 1]
            top = r1[0] + r2[0]
            zsum = jnp.zeros((SUBLANES, tt), F32)
            for c in cand + [singles]:
                zsum = zsum + jnp.where(c >= tau, jnp.exp(c - top), 0.0)
            z = jnp.sum(zsum, axis=0, keepdims=True)
            c_scr[h] = tau - s1
            e1_scr[h] = jnp.exp(s1 - r1[0])
            e2 = jnp.exp(s2 - r2[0]) / z
            for lg in range(n_lg):
                ls = slice(lg * LANES, (lg + 1) * LANES)
                s2_scr[h, lg] = s2[:, ls].reshape(KEY_TILES, SUBLANES, LANES)
                e2_scr[h, lg] = e2[:, ls].reshape(KEY_TILES, SUBLANES, LANES)
            return carry

        lax.fori_loop(0, PEER_HEADS, head, 0)

        acc_scr[...] = jnp.zeros_like(acc_scr)
        _activations(u_ref[...], h2_scr, s_a, slice(None), n_lg)

    def step(s_w, s_r):
        a0 = pl.multiple_of((e - 1) * na, na)
        for h in range(PEER_HEADS):
            cslab = c_scr[h, pl.ds(a0, na), :]
            eslab = e1_scr[h, pl.ds(a0, na), :]
            for r in range(na):
                cb = jnp.broadcast_to(cslab[r:r + 1, :], (SUBLANES, tt))
                eb = jnp.broadcast_to(eslab[r:r + 1, :], (SUBLANES, tt))
                for lg in range(n_lg):
                    cb_scr[h, r, lg] = cb[:, lg * LANES:(lg + 1) * LANES]
                    eb_scr[h, r, lg] = eb[:, lg * LANES:(lg + 1) * LANES]

        halves = [tuple(range(lo, min(lo + 2, n_lg))) for lo in range(0, n_lg, 2)]
        groups = [(half, ai, lg) for half in halves for ai in range(na) for lg in half]
        act_at = [ACT_LAG + p * (len(groups) // ACT_PIECES) for p in range(ACT_PIECES)]
        rows_per = na * PEER_NKEYS // ACT_PIECES
        ga = None
        for gi, (half, ai, lg) in enumerate(groups):
            if s_w is not None and gi in act_at:
                p = act_at.index(gi)
                start = p * rows_per
                if ga is not None:
                    probe = jnp.max(ga[0:SUBLANES, :])
                    other = (start + rows_per) % (na * PEER_NKEYS)
                    start = jnp.where(probe != probe, other, start)
                start = pl.multiple_of(start, PACKED_ROWS)
                _activations(u_ref[pl.ds(start, rows_per), :], h2_scr, s_w,
                             pl.ds(start, rows_per), n_lg)
            gate = jnp.zeros((KEY_TILES, SUBLANES, LANES), F32)
            for h in range(PEER_HEADS):
                hit = s2_scr[h, lg] >= cb_scr[h, ai, lg][None]
                gate = gate + jnp.where(hit, e2_scr[h, lg], 0.0) * eb_scr[h, ai, lg][None]
            act = _gelu_tanh(s_r[lg, ai * PEER_NKEYS:(ai + 1) * PEER_NKEYS, :])
            ga = gate.reshape(PEER_NKEYS, LANES) * act
            ga_scr[ai * PEER_NKEYS:(ai + 1) * PEER_NKEYS, lg * LANES:(lg + 1) * LANES] = ga.astype(BF16)
            if ai == na - 1 and lg == half[-1]:
                hs = slice(half[0] * LANES, (half[-1] + 1) * LANES)
                acc_scr[:, hs] += jnp.dot(vt_ref[...], ga_scr[:, hs], preferred_element_type=F32)

    assert n_chunks % 2 == 0

    @pl.when(e % 2 == 1)
    def _odd():
        step(s_b, s_a)

    @pl.when((e > 0) & (e % 2 == 0) & (e < n_chunks))
    def _even():
        step(s_a, s_b)

    @pl.when(e == n_chunks)
    def _last():
        step(None, s_b)
        y = acc_scr[...].T
        o_ref[...] = _ln(ALPHA * x1_ref[...] + g2_ref[0] * y) * lng_ref[...] + lnb_ref[...]


def _peer(x1_flat, sc, sh, g2, wts, layer, tt, per_token, tiles_per_seq):
    t = x1_flat.shape[0]
    wqt_b, keys_b, u_b, vt_b, ln2_g, ln2_b = wts
    na = 8
    ec = na * PEER_NKEYS
    n_chunks = PEER_N // ec
    n_lg = tt // LANES
    mod = _mod_spec(per_token, tt, tiles_per_seq)
    return pl.pallas_call(
        functools.partial(_peer_kernel, tt=tt, na=na, n_chunks=n_chunks),
        grid=(t // tt, n_chunks + 1),
        in_specs=[
            pl.BlockSpec((tt, D_MODEL), lambda i, e: (i, 0)),
            mod, mod, mod,
            pl.BlockSpec((None, PEER_HEADS * PEER_DQ, D_MODEL), lambda i, e: (layer, 0, 0)),
            pl.BlockSpec((None, 2, PEER_NKEYS, PEER_DHALF), lambda i, e: (layer, 0, 0, 0)),
            pl.BlockSpec((None, ec, D_MODEL),
                         lambda i, e: (layer, jnp.minimum(e, n_chunks - 1), 0)),
            pl.BlockSpec((None, D_MODEL, ec), lambda i, e: (layer, 0, jnp.maximum(e - 1, 0))),
            pl.BlockSpec((None, 1, D_MODEL), lambda i, e: (layer, 0, 0)),
            pl.BlockSpec((None, 1, D_MODEL), lambda i, e: (layer, 0, 0)),
        ],
        out_specs=pl.BlockSpec((tt, D_MODEL), lambda i, e: (i, 0)),
        out_shape=jax.ShapeDtypeStruct((t, D_MODEL), F32),
        scratch_shapes=[
            pltpu.VMEM((D_MODEL, tt), BF16),
            pltpu.VMEM((PEER_HEADS * PEER_DQ, tt), BF16),
            pltpu.VMEM((PEER_HEADS, PEER_NKEYS, tt), F32),
            pltpu.VMEM((PEER_HEADS, PEER_NKEYS, tt), F32),
            pltpu.VMEM((PEER_HEADS, n_lg, KEY_TILES, SUBLANES, LANES), F32),
            pltpu.VMEM((PEER_HEADS, n_lg, KEY_TILES, SUBLANES, LANES), F32),
            pltpu.VMEM((PEER_HEADS, na, n_lg, SUBLANES, LANES), F32),
            pltpu.VMEM((PEER_HEADS, na, n_lg, SUBLANES, LANES), F32),
            pltpu.VMEM((n_lg, ec, LANES), F32),
            pltpu.VMEM((n_lg, ec, LANES), F32),
            pltpu.VMEM((ec, tt), BF16),
            pltpu.VMEM((D_MODEL, tt), F32),
        ],
        compiler_params=_params("arbitrary", "arbitrary"),
        name="peer",
    )(x1_flat, sc, sh, g2, wqt_b, keys_b, u_b, vt_b, ln2_g, ln2_b)


def _rope_tables(pos0, seq):
    half = DK_RET // 2
    inv = ROPE_BASE ** (-jnp.arange(half, dtype=F32) / half)
    ang = (pos0 + jnp.arange(seq)).astype(F32)[:, None] * inv[None, :]
    cos = jnp.cos(ang)
    sin = jnp.sin(ang)
    return jnp.concatenate([cos, cos], axis=-1), jnp.concatenate([-sin, sin], axis=-1)


def _trunk(x, mods, pool_hist, ret_state, pos0, wts, tiles):
    bsz, seq, _ = x.shape
    tt_mix, tt_peer = tiles
    per_token = seq < tt_peer
    cos2, sin2 = _rope_tables(pos0, seq)
    hist16 = jnp.pad(pool_hist, ((0, 0), (0, 0), (HIST_ROWS - POOL_HIST, 0), (0, 0)))
    new_hist, new_state = [], []
    for l in range(DEPTH):
        chunks = [mods[l, :, k * D_MODEL:(k + 1) * D_MODEL] for k in range(6)]
        seq_rows = [c[:, None, :] for c in chunks]
        if per_token:
            tok_rows = [jnp.repeat(c, seq, axis=0)[None] for c in chunks]
        else:
            tok_rows = seq_rows
        _, _, _, sh2, sc2, g2 = tok_rows
        sh1, sc1, g1 = seq_rows[:3]
        x1, s_l, hist_l = _mix(x, sc1, sh1, g1, cos2, sin2, hist16[l], ret_state[l],
                               (wts["w_in"], wts["w_grp"], wts["pool_scale"], wts["w_bp"],
                                wts["w_br"], wts["w_out"], wts["ln1_g"], wts["ln1_b"]),
                               l, tt_mix, pos0)
        x2 = _peer(x1.reshape(bsz * seq, D_MODEL), sc2, sh2, g2,
                   (wts["w_qt"], wts["keys"], wts["u"], wts["vt"], wts["ln2_g"], wts["ln2_b"]),
                   l, tt_peer, per_token, seq // tt_peer if not per_token else 1)
        x = x2.reshape(bsz, seq, D_MODEL)
        new_hist.append(hist_l[:, HIST_ROWS - POOL_HIST:, :])
        new_state.append(s_l)
    return x, jnp.stack(new_hist), jnp.stack(new_state)


def kernel(x_prompt, x_sample, cache_pool, state_ret, c_prompt, c_sample, w_ada, b_ada, w_in,
           w_pool_grp, pool_scale, w_branch_pool, w_branch_ret, w_out, ln1_g, ln1_b, w_peer_q,
           peer_sub_keys, peer_u, peer_v, ln2_g, ln2_b):
    n_prompt = x_prompt.shape[0]
    n_sample = x_sample.shape[0]
    wts = {
        "w_in": w_in.astype(BF16),
        "w_grp": w_pool_grp.astype(BF16),
        "pool_scale": pool_scale[:, None, :],
        "w_bp": w_branch_pool.astype(BF16),
        "w_br": w_branch_ret.astype(BF16),
        "w_out": w_out.astype(BF16),
        "ln1_g": ln1_g[:, None, :],
        "ln1_b": ln1_b[:, None, :],
        "w_qt": jnp.swapaxes(w_peer_q, 1, 2).astype(BF16),
        "keys": peer_sub_keys.astype(BF16),
        "u": peer_u.astype(BF16),
        "vt": jnp.swapaxes(peer_v, 1, 2).astype(BF16),
        "ln2_g": ln2_g[:, None, :],
        "ln2_b": ln2_b[:, None, :],
    }
    n_seq = n_prompt + n_sample
    c_pad = jnp.pad(jnp.concatenate([c_prompt, c_sample], axis=0), ((0, 16 - n_seq), (0, 0)))
    mods = _ada(c_pad, w_ada, b_ada)
    zero_hist = jnp.zeros((DEPTH, n_prompt, POOL_HIST, D_POOL), x_prompt.dtype)
    zero_state = jnp.zeros((DEPTH, n_prompt, N_RET_HEADS, DK_RET, DK_RET), F32)
    y_p, pool_p, ret_p = _trunk(x_prompt, mods[:, :n_prompt], zero_hist, zero_state, 0, wts,
                                (256, 512))
    y_s, pool_s, ret_s = _trunk(x_sample, mods[:, n_prompt:n_seq], cache_pool, state_ret,
                                PAST_LEN, wts, (32, 256))
    return (y_p, y_s, pool_p, ret_p.astype(state_ret.dtype), pool_s, ret_s.astype(state_ret.dtype))
```

```python
import functools
import math

import jax
import jax.numpy as jnp
from jax import lax
from jax.experimental import pallas as pl
from jax.experimental.pallas import tpu as pltpu

F32 = jnp.float32
BF16 = jnp.bfloat16

D_MODEL = 1024
DEPTH = 2
PAST_LEN = 2048
POOL_WINDOWS = (2, 4, 8, 16)
D_POOL = 512
POOL_GROUP = 128
POOL_HIST = 15
HIST_ROWS = 16
N_RET_HEADS = 8
DK_RET = 128
D_RET = N_RET_HEADS * DK_RET
RET_SCALE = DK_RET ** -0.5
ROPE_BASE = 10000.0
PEER_HEADS = 8
PEER_NKEYS = 128
PEER_N = PEER_NKEYS * PEER_NKEYS
PEER_DQ = 256
PEER_DHALF = 128
PEER_TOPK = 16
LN_EPS = 1e-5
ALPHA = (2 * DEPTH) ** 0.25
OFF_Q = D_POOL
OFF_K = OFF_Q + D_RET
OFF_V = OFF_K + D_RET
OFF_G = OFF_V + D_RET
OFF_BG = OFF_G + D_RET
D_IN = OFF_BG + 2 * D_MODEL
LOG_G = tuple(math.log(1.0 - 2.0 ** (-5.0 - h)) for h in range(N_RET_HEADS))
LANES = 128
SUBLANES = 8
KEY_TILES = PEER_NKEYS // SUBLANES
PACKED_ROWS = 16
ACT_PIECES = 2
ACT_LAG = 1
VMEM_LIMIT = 60 * 1024 * 1024
NEG_INF = float("-inf")


def _ln(x):
    mu = jnp.mean(x, axis=-1, keepdims=True)
    xc = x - mu
    var = jnp.mean(xc * xc, axis=-1, keepdims=True)
    return xc * lax.rsqrt(var + LN_EPS)


def _gelu_tanh(x):
    a0 = -2.0 * math.sqrt(2.0 / math.pi) * math.log2(math.e)
    return x / (1.0 + jnp.exp2(x * (a0 + (a0 * 0.044715) * (x * x))))


def _params(*sem):
    return pltpu.CompilerParams(dimension_semantics=sem, vmem_limit_bytes=VMEM_LIMIT)


def _ada_kernel(c_ref, w_ref, b_ref, o_ref):
    c = c_ref[...]
    a = (c * jax.nn.sigmoid(c)).astype(BF16)
    o_ref[...] = jnp.dot(a, w_ref[...].astype(BF16), preferred_element_type=F32) + b_ref[...]


def _ada(c_pad, w_ada, b_ada):
    rows = c_pad.shape[0]
    n_out = w_ada.shape[-1]
    ct = 1536
    return pl.pallas_call(
        _ada_kernel,
        grid=(DEPTH, n_out // ct),
        in_specs=[
            pl.BlockSpec((rows, D_MODEL), lambda l, j: (0, 0)),
            pl.BlockSpec((None, D_MODEL, ct), lambda l, j: (l, 0, j)),
            pl.BlockSpec((None, 1, ct), lambda l, j: (l, 0, j)),
        ],
        out_specs=pl.BlockSpec((None, rows, ct), lambda l, j: (l, 0, j)),
        out_shape=jax.ShapeDtypeStruct((DEPTH, rows, n_out), F32),
        compiler_params=_params("arbitrary", "arbitrary"),
        name="ada",
    )(c_pad, w_ada, b_ada.reshape(DEPTH, 1, n_out))


def _mod_spec(per_token, tt, tiles_per_seq):
    if per_token:
        return pl.BlockSpec((1, tt, D_MODEL), lambda i, *_: (0, i, 0))
    return pl.BlockSpec((1, 1, D_MODEL), lambda i, *_: (i // tiles_per_seq, 0, 0))


def _mix_kernel(x_ref, sc_ref, sh_ref, g1_ref, cos_ref, sin_ref, hist_ref, s0_ref, win_ref,
                wgrp_ref, pscale_ref, wbp_ref, wbr_ref, wout_ref, lng_ref, lnb_ref,
                x1_ref, sfin_ref, hout_ref,
                proj_ref, state_scr, ext_scr, decay_scr, xi_scr, zeta_scr, retg_scr, pool_scr,
                *, tt, pos0, n_tiles):
    j = pl.program_id(1)

    h_in = (_ln(x_ref[0]) * (1.0 + sc_ref[0]) + sh_ref[0]).astype(BF16)
    proj_ref[...] = jnp.dot(h_in, win_ref[...], preferred_element_type=F32)

    @pl.when(j == 0)
    def _init():
        state_scr[...] = s0_ref[0]
        ext_scr[0:HIST_ROWS, :] = hist_ref[0]
        row = lax.broadcasted_iota(jnp.int32, (tt, tt), 0)
        col = lax.broadcasted_iota(jnp.int32, (tt, tt), 1)
        causal = row >= col
        diff = jnp.where(causal, (row - col).astype(F32), 0.0)
        rowl = lax.broadcasted_iota(jnp.int32, (tt, LANES), 0).astype(F32)
        for h in range(N_RET_HEADS):
            decay_scr[h] = jnp.where(causal, jnp.exp(LOG_G[h] * diff), 0.0)
            xi_scr[h] = jnp.exp(LOG_G[h] * (rowl + 1.0))
            zeta_scr[h] = jnp.exp(LOG_G[h] * (tt - 1.0 - rowl))

    p = proj_ref[:,0:D_POOL]
    ext_scr[HIST_ROWS:HIST_ROWS + tt, :] = p
    pos = pos0 + j * tt + lax.broadcasted_iota(jnp.int32, (tt, POOL_GROUP), 0)
    for g, w in enumerate(POOL_WINDOWS):
        cs = slice(g * POOL_GROUP, (g + 1) * POOL_GROUP)
        ws = ext_scr[HIST_ROWS:HIST_ROWS + tt, cs]
        for d in range(1, w):
            ws = ws + ext_scr[HIST_ROWS - d:HIST_ROWS - d + tt, cs]
        cnt = jnp.minimum(pos + 1, w).astype(F32)
        pooled = ws / cnt - p[:, cs]
        mixed = jnp.dot(pooled.astype(BF16), wgrp_ref[g], preferred_element_type=F32)
        pool_scr[:, cs] = (mixed * pscale_ref[:, cs]).astype(BF16)
    ext_scr[0:HIST_ROWS, :] = ext_scr[tt:tt + HIST_ROWS, :]

    cosv = cos_ref[...]
    sinv = sin_ref[...]
    for h in range(N_RET_HEADS):
        hs = slice(h * DK_RET, (h + 1) * DK_RET)
        q = proj_ref[:,OFF_Q + h * DK_RET:OFF_Q + (h + 1) * DK_RET]
        k = proj_ref[:,OFF_K + h * DK_RET:OFF_K + (h + 1) * DK_RET]
        v = proj_ref[:,OFF_V + h * DK_RET:OFF_V + (h + 1) * DK_RET].astype(BF16)
        g = proj_ref[:,OFF_G + h * DK_RET:OFF_G + (h + 1) * DK_RET]
        qr = q * cosv + pltpu.roll(q, DK_RET // 2, 1) * sinv
        kr = (k * cosv + pltpu.roll(k, DK_RET // 2, 1) * sinv) * RET_SCALE
        qb = qr.astype(BF16)
        scores = lax.dot_general(qb, kr.astype(BF16), (((1,), (1,)), ((), ())),
                                 preferred_element_type=F32) * decay_scr[h]
        intra = jnp.dot(scores.astype(BF16), v, preferred_element_type=F32)
        s_prev = state_scr[h]
        cross = jnp.dot(qb, s_prev.astype(BF16), preferred_element_type=F32) * xi_scr[h]
        kz = (kr * zeta_scr[h]).astype(BF16)
        kv = lax.dot_general(kz, v, (((0,), (0,)), ((), ())), preferred_element_type=F32)
        state_scr[h] = math.exp(LOG_G[h] * tt) * s_prev + kv
        retg_scr[:, hs] = (_ln(intra + cross) * (g * jax.nn.sigmoid(g))).astype(BF16)

    bgp = jax.nn.sigmoid(proj_ref[:,OFF_BG:OFF_BG + D_MODEL])
    bgr = jax.nn.sigmoid(proj_ref[:,OFF_BG + D_MODEL:OFF_BG + 2 * D_MODEL])
    pb = jnp.dot(pool_scr[...], wbp_ref[...], preferred_element_type=F32)
    rb = jnp.dot(retg_scr[...], wbr_ref[...], preferred_element_type=F32)
    merged = (bgp * pb + bgr * rb).astype(BF16)
    z = jnp.dot(merged, wout_ref[...], preferred_element_type=F32)
    x1_ref[0] = _ln(ALPHA * x_ref[0] + g1_ref[0] * z) * lng_ref[...] + lnb_ref[...]

    @pl.when(j == n_tiles - 1)
    def _fin():
        sfin_ref[0] = state_scr[...]
        hout_ref[0] = ext_scr[0:HIST_ROWS, :]


def _mix(x, sc1, sh1, g1, cos2, sin2, hist16, s0, wts, layer, tt, pos0):
    bsz, seq, _ = x.shape
    n_tiles = seq // tt
    w_in_b, wgrp_b, pscale, wbp_b, wbr_b, wout_b, ln1_g, ln1_b = wts
    const2 = lambda b, j: (layer, 0, 0)
    once = pl.Buffered(1)
    seq_row = pl.BlockSpec((1, 1, D_MODEL), lambda b, j: (b, 0, 0))
    return pl.pallas_call(
        functools.partial(_mix_kernel, tt=tt, pos0=pos0, n_tiles=n_tiles),
        grid=(bsz, n_tiles),
        in_specs=[
            pl.BlockSpec((1, tt, D_MODEL), lambda b, j: (b, j, 0)),
            seq_row, seq_row, seq_row,
            pl.BlockSpec((tt, LANES), lambda b, j: (j, 0)),
            pl.BlockSpec((tt, LANES), lambda b, j: (j, 0)),
            pl.BlockSpec((1, HIST_ROWS, D_POOL), lambda b, j: (b, 0, 0)),
            pl.BlockSpec((1, N_RET_HEADS, DK_RET, DK_RET), lambda b, j: (b, 0, 0, 0)),
            pl.BlockSpec((None, D_MODEL, D_IN), const2, pipeline_mode=once),
            pl.BlockSpec((None, len(POOL_WINDOWS), POOL_GROUP, POOL_GROUP),
                         lambda b, j: (layer, 0, 0, 0)),
            pl.BlockSpec((None, 1, D_POOL), const2),
            pl.BlockSpec((None, D_POOL, D_MODEL), const2, pipeline_mode=once),
            pl.BlockSpec((None, D_RET, D_MODEL), const2, pipeline_mode=once),
            pl.BlockSpec((None, D_MODEL, D_MODEL), const2, pipeline_mode=once),
            pl.BlockSpec((None, 1, D_MODEL), const2),
            pl.BlockSpec((None, 1, D_MODEL), const2),
        ],
        out_specs=[
            pl.BlockSpec((1, tt, D_MODEL), lambda b, j: (b, j, 0)),
            pl.BlockSpec((1, N_RET_HEADS, DK_RET, DK_RET), lambda b, j: (b, 0, 0, 0)),
            pl.BlockSpec((1, HIST_ROWS, D_POOL), lambda b, j: (b, 0, 0)),
        ],
        out_shape=[
            jax.ShapeDtypeStruct((bsz, seq, D_MODEL), F32),
            jax.ShapeDtypeStruct((bsz, N_RET_HEADS, DK_RET, DK_RET), F32),
            jax.ShapeDtypeStruct((bsz, HIST_ROWS, D_POOL), F32),
        ],
        scratch_shapes=[
            pltpu.VMEM((tt, D_IN), F32),
            pltpu.VMEM((N_RET_HEADS, DK_RET, DK_RET), F32),
            pltpu.VMEM((HIST_ROWS + tt, D_POOL), F32),
            pltpu.VMEM((N_RET_HEADS, tt, tt), F32),
            pltpu.VMEM((N_RET_HEADS, tt, LANES), F32),
            pltpu.VMEM((N_RET_HEADS, tt, LANES), F32),
            pltpu.VMEM((tt, D_RET), BF16),
            pltpu.VMEM((tt, D_POOL), BF16),
        ],
        compiler_params=_params("arbitrary", "arbitrary"),
        name="mix",
    )(x, sc1, sh1, g1, cos2, sin2, hist16, s0, w_in_b, wgrp_b, pscale, wbp_b, wbr_b, wout_b,
      ln1_g, ln1_b)


def _sort_tiles_desc(tiles):
    a = list(tiles)
    n = len(a)
    k = 2
    while k <= n:
        j = k // 2
        while j >= 1:
            for i in range(n):
                l = i ^ j
                if l > i:
                    hi = jnp.maximum(a[i], a[l])
                    lo = jnp.minimum(a[i], a[l])
                    a[i], a[l] = (hi, lo) if (i & k) == 0 else (lo, hi)
            j //= 2
        k *= 2
    return a


def _merge_top16(lists, extra=None):
    cur = list(lists)
    rows = []
    for r in range(PEER_TOPK):
        m = jnp.max(cur[0], axis=0, keepdims=True)
        if extra is not None:
            m = jnp.maximum(m, jnp.max(extra, axis=0, keepdims=True))
        rows.append(m)
        if r + 1 < PEER_TOPK:
            hit = cur[0] == m
            cur = [jnp.where(hit, cur[j + 1], cur[j]) for j in range(PEER_TOPK - 1 - r)]
            if extra is not None:
                extra = jnp.where(extra == m, NEG_INF, extra)
    return rows


def _top16_rows(s):
    tiles = [s[k * SUBLANES:(k + 1) * SUBLANES, :] for k in range(KEY_TILES)]
    return _merge_top16(_sort_tiles_desc(tiles))


def _stack_rows(rows, tt):
    ridx = lax.broadcasted_iota(jnp.int32, (len(rows), tt), 0)
    out = jnp.zeros((len(rows), tt), F32)
    for i, r in enumerate(rows):
        out = jnp.where(ridx == i, r, out)
    return out


def _activations(u_rows, h2_scr, s_dst, rows, n_lg):
    s_val = jnp.dot(u_rows, h2_scr[...], preferred_element_type=F32)
    for lg in range(n_lg):
        s_dst[lg, rows, :] = s_val[:, lg * LANES:(lg + 1) * LANES]


def _peer_kernel(x1_ref, sc_ref, sh_ref, g2_ref, wqt_ref, keys_ref, u_ref, vt_ref,
                 lng_ref, lnb_ref, o_ref,
                 h2_scr, c_scr, e1_scr, s2_scr, e2_scr, cb_scr, eb_scr, s_a, s_b, ga_scr, acc_scr,
                 *, tt, na, n_chunks):
    e = pl.program_id(1)
    n_lg = tt // LANES

    @pl.when(e == 0)
    def _prep():
        h2 = _ln(x1_ref[...]) * (1.0 + sc_ref[0]) + sh_ref[0]
        h2_scr[...] = h2.T.astype(BF16)
        qt = jnp.dot(wqt_ref[...], h2_scr[...], preferred_element_type=F32)
        qt_scr = ga_scr
        qt_scr[...] = qt.astype(BF16)

        def head(h, carry):
            base = pl.multiple_of(h * PEER_DQ, PEER_DQ)
            s1 = jnp.dot(keys_ref[0], qt_scr[pl.ds(base, PEER_DHALF), :],
                         preferred_element_type=F32)
            s2 = jnp.dot(keys_ref[1], qt_scr[pl.ds(base + PEER_DHALF, PEER_DHALF), :],
                         preferred_element_type=F32)
            r1 = _top16_rows(s1)
            r2 = _top16_rows(s2)
            v2_lo = _stack_rows(r2[:SUBLANES], tt)
            v2_hi = _stack_rows(r2[SUBLANES:], tt)
            cand = [r1[i] + v2_lo for i in range(PEER_TOPK)]
            singles = r1[0] + v2_hi
            tau = _merge_top16(cand, singles)[PEER_TOPK - 1]
            top = r1[0] + r2[0]
            zsum = jnp.zeros((SUBLANES, tt), F32)
            for c in cand + [singles]:
                zsum = zsum + jnp.where(c >= tau, jnp.exp(c - top), 0.0)
            z = jnp.sum(zsum, axis=0, keepdims=True)
            c_scr[h] = tau - s1
            e1_scr[h] = jnp.exp(s1 - r1[0])
            e2 = jnp.exp(s2 - r2[0]) / z
            for lg in range(n_lg):
                ls = slice(lg * LANES, (lg + 1) * LANES)
                s2_scr[h, lg] = s2[:, ls].reshape(KEY_TILES, SUBLANES, LANES)
                e2_scr[h, lg] = e2[:, ls].reshape(KEY_TILES, SUBLANES, LANES)
            return carry

        lax.fori_loop(0, PEER_HEADS, head, 0)

        acc_scr[...] = jnp.zeros_like(acc_scr)
        _activations(u_ref[...], h2_scr, s_a, slice(None), n_lg)

    def step(s_w, s_r):
        a0 = pl.multiple_of((e - 1) * na, na)
        for h in range(PEER_HEADS):
            cslab = c_scr[h, pl.ds(a0, na), :]
            eslab = e1_scr[h, pl.ds(a0, na), :]
            for r in range(na):
                cb = jnp.broadcast_to(cslab[r:r + 1, :], (SUBLANES, tt))
                eb = jnp.broadcast_to(eslab[r:r + 1, :], (SUBLANES, tt))
                for lg in range(n_lg):
                    cb_scr[h, r, lg] = cb[:, lg * LANES:(lg + 1) * LANES]
                    eb_scr[h, r, lg] = eb[:, lg * LANES:(lg + 1) * LANES]

        halves = [tuple(range(lo, min(lo + 2, n_lg))) for lo in range(0, n_lg, 2)]
        groups = [(half, ai, lg) for half in halves for ai in range(na) for lg in half]
        act_at = [ACT_LAG + p * (len(groups) // ACT_PIECES) for p in range(ACT_PIECES)]
        rows_per = na * PEER_NKEYS // ACT_PIECES
        ga = None
        for gi, (half, ai, lg) in enumerate(groups):
            if s_w is not None and gi in act_at:
                p = act_at.index(gi)
                start = p * rows_per
                if ga is not None:
                    probe = jnp.max(ga[0:SUBLANES, :])
                    other = (start + rows_per) % (na * PEER_NKEYS)
                    start = jnp.where(probe != probe, other, start)
                start = pl.multiple_of(start, PACKED_ROWS)
                _activations(u_ref[pl.ds(start, rows_per), :], h2_scr, s_w,
                             pl.ds(start, rows_per), n_lg)
            gate = jnp.zeros((KEY_TILES, SUBLANES, LANES), F32)
            for h in range(PEER_HEADS):
                hit = s2_scr[h, lg] >= cb_scr[h, ai, lg][None]
                gate = gate + jnp.where(hit, e2_scr[h, lg], 0.0) * eb_scr[h, ai, lg][None]
            act = _gelu_tanh(s_r[lg, ai * PEER_NKEYS:(ai + 1) * PEER_NKEYS, :])
            ga = gate.reshape(PEER_NKEYS, LANES) * act
            ga_scr[ai * PEER_NKEYS:(ai + 1) * PEER_NKEYS, lg * LANES:(lg + 1) * LANES] = ga.astype(BF16)
            if ai == na - 1 and lg == half[-1]:
                hs = slice(half[0] * LANES, (half[-1] + 1) * LANES)
                acc_scr[:, hs] += jnp.dot(vt_ref[...], ga_scr[:, hs], preferred_element_type=F32)

    assert n_chunks % 2 == 0

    @pl.when(e % 2 == 1)
    def _odd():
        step(s_b, s_a)

    @pl.when((e > 0) & (e % 2 == 0) & (e < n_chunks))
    def _even():
        step(s_a, s_b)

    @pl.when(e == n_chunks)
    def _last():
        step(None, s_b)
        y = acc_scr[...].T
        o_ref[...] = _ln(ALPHA * x1_ref[...] + g2_ref[0] * y) * lng_ref[...] + lnb_ref[...]


def _peer(x1_flat, sc, sh, g2, wts, layer, tt, per_token, tiles_per_seq):
    t = x1_flat.shape[0]
    wqt_b, keys_b, u_b, vt_b, ln2_g, ln2_b = wts
    na = 16
    ec = na * PEER_NKEYS
    assert ec == PEER_HEADS * PEER_DQ
    n_chunks = PEER_N // ec
    n_lg = tt // LANES
    mod = _mod_spec(per_token, tt, tiles_per_seq)
    return pl.pallas_call(
        functools.partial(_peer_kernel, tt=tt, na=na, n_chunks=n_chunks),
        grid=(t // tt, n_chunks + 1),
        in_specs=[
            pl.BlockSpec((tt, D_MODEL), lambda i, e: (i, 0)),
            mod, mod, mod,
            pl.BlockSpec((None, PEER_HEADS * PEER_DQ, D_MODEL), lambda i, e: (layer, 0, 0)),
            pl.BlockSpec((None, 2, PEER_NKEYS, PEER_DHALF), lambda i, e: (layer, 0, 0, 0)),
            pl.BlockSpec((None, ec, D_MODEL),
                         lambda i, e: (layer, jnp.minimum(e, n_chunks - 1), 0)),
            pl.BlockSpec((None, D_MODEL, ec), lambda i, e: (layer, 0, jnp.maximum(e - 1, 0))),
            pl.BlockSpec((None, 1, D_MODEL), lambda i, e: (layer, 0, 0)),
            pl.BlockSpec((None, 1, D_MODEL), lambda i, e: (layer, 0, 0)),
        ],
        out_specs=pl.BlockSpec((tt, D_MODEL), lambda i, e: (i, 0)),
        out_shape=jax.ShapeDtypeStruct((t, D_MODEL), F32),
        scratch_shapes=[
            pltpu.VMEM((D_MODEL, tt), BF16),
            pltpu.VMEM((PEER_HEADS, PEER_NKEYS, tt), F32),
            pltpu.VMEM((PEER_HEADS, PEER_NKEYS, tt), F32),
            pltpu.VMEM((PEER_HEADS, n_lg, KEY_TILES, SUBLANES, LANES), F32),
            pltpu.VMEM((PEER_HEADS, n_lg, KEY_TILES, SUBLANES, LANES), F32),
            pltpu.VMEM((PEER_HEADS, na, n_lg, SUBLANES, LANES), F32),
            pltpu.VMEM((PEER_HEADS, na, n_lg, SUBLANES, LANES), F32),
            pltpu.VMEM((n_lg, ec, LANES), F32),
            pltpu.VMEM((n_lg, ec, LANES), F32),
            pltpu.VMEM((ec, tt), BF16),
            pltpu.VMEM((D_MODEL, tt), F32),
        ],
        compiler_params=_params("arbitrary", "arbitrary"),
        name="peer",
    )(x1_flat, sc, sh, g2, wqt_b, keys_b, u_b, vt_b, ln2_g, ln2_b)


def _rope_tables(pos0, seq):
    half = DK_RET // 2
    inv = ROPE_BASE ** (-jnp.arange(half, dtype=F32) / half)
    ang = (pos0 + jnp.arange(seq)).astype(F32)[:, None] * inv[None, :]
    cos = jnp.cos(ang)
    sin = jnp.sin(ang)
    return jnp.concatenate([cos, cos], axis=-1), jnp.concatenate([-sin, sin], axis=-1)


def _trunk(x, mods, pool_hist, ret_state, pos0, wts, tiles):
    bsz, seq, _ = x.shape
    tt_mix, tt_peer = tiles
    per_token = seq < tt_peer
    cos2, sin2 = _rope_tables(pos0, seq)
    hist16 = jnp.pad(pool_hist, ((0, 0), (0, 0), (HIST_ROWS - POOL_HIST, 0), (0, 0)))
    new_hist, new_state = [], []
    for l in range(DEPTH):
        chunks = [mods[l, :, k * D_MODEL:(k + 1) * D_MODEL] for k in range(6)]
        seq_rows = [c[:, None, :] for c in chunks]
        if per_token:
            tok_rows = [jnp.repeat(c, seq, axis=0)[None] for c in chunks]
        else:
            tok_rows = seq_rows
        _, _, _, sh2, sc2, g2 = tok_rows
        sh1, sc1, g1 = seq_rows[:3]
        x1, s_l, hist_l = _mix(x, sc1, sh1, g1, cos2, sin2, hist16[l], ret_state[l],
                               (wts["w_in"], wts["w_grp"], wts["pool_scale"], wts["w_bp"],
                                wts["w_br"], wts["w_out"], wts["ln1_g"], wts["ln1_b"]),
                               l, tt_mix, pos0)
        x2 = _peer(x1.reshape(bsz * seq, D_MODEL), sc2, sh2, g2,
                   (wts["w_qt"], wts["keys"], wts["u"], wts["vt"], wts["ln2_g"], wts["ln2_b"]),
                   l, tt_peer, per_token, seq // tt_peer if not per_token else 1)
        x = x2.reshape(bsz, seq, D_MODEL)
        new_hist.append(hist_l[:, HIST_ROWS - POOL_HIST:, :])
        new_state.append(s_l)
    return x, jnp.stack(new_hist), jnp.stack(new_state)


def kernel(x_prompt, x_sample, cache_pool, state_ret, c_prompt, c_sample, w_ada, b_ada, w_in,
           w_pool_grp, pool_scale, w_branch_pool, w_branch_ret, w_out, ln1_g, ln1_b, w_peer_q,
           peer_sub_keys, peer_u, peer_v, ln2_g, ln2_b):
    n_prompt = x_prompt.shape[0]
    n_sample = x_sample.shape[0]
    wts = {
        "w_in": w_in.astype(BF16),
        "w_grp": w_pool_grp.astype(BF16),
        "pool_scale": pool_scale[:, None, :],
        "w_bp": w_branch_pool.astype(BF16),
        "w_br": w_branch_ret.astype(BF16),
        "w_out": w_out.astype(BF16),
        "ln1_g": ln1_g[:, None, :],
        "ln1_b": ln1_b[:, None, :],
        "w_qt": jnp.swapaxes(w_peer_q, 1, 2).astype(BF16),
        "keys": peer_sub_keys.astype(BF16),
        "u": peer_u.astype(BF16),
        "vt": jnp.swapaxes(peer_v, 1, 2).astype(BF16),
        "ln2_g": ln2_g[:, None, :],
        "ln2_b": ln2_b[:, None, :],
    }
    n_seq = n_prompt + n_sample
    c_pad = jnp.pad(jnp.concatenate([c_prompt, c_sample], axis=0), ((0, 16 - n_seq), (0, 0)))
    mods = _ada(c_pad, w_ada, b_ada)
    zero_hist = jnp.zeros((DEPTH, n_prompt, POOL_HIST, D_POOL), x_prompt.dtype)
    zero_state = jnp.zeros((DEPTH, n_prompt, N_RET_HEADS, DK_RET, DK_RET), F32)
    y_p, pool_p, ret_p = _trunk(x_prompt, mods[:, :n_prompt], zero_hist, zero_state, 0, wts,
                                (256, 512))
    y_s, pool_s, ret_s = _trunk(x_sample, mods[:, n_prompt:n_seq], cache_pool, state_ret,
                                PAST_LEN, wts, (32, 256))
    return (y_p, y_s, pool_p, ret_p.astype(state_ret.dtype), pool_s, ret_s.astype(state_ret.dtype))
```

```python
import functools
import math

import jax
import jax.numpy as jnp
from jax import lax
from jax.experimental import pallas as pl
from jax.experimental.pallas import tpu as pltpu

F32 = jnp.float32
BF16 = jnp.bfloat16

D_MODEL = 1024
DEPTH = 2
PAST_LEN = 2048
POOL_WINDOWS = (2, 4, 8, 16)
D_POOL = 512
POOL_GROUP = 128
POOL_HIST = 15
HIST_ROWS = 16
N_RET_HEADS = 8
DK_RET = 128
D_RET = N_RET_HEADS * DK_RET
RET_SCALE = DK_RET ** -0.5
ROPE_BASE = 10000.0
PEER_HEADS = 8
PEER_NKEYS = 128
PEER_N = PEER_NKEYS * PEER_NKEYS
PEER_DQ = 256
PEER_DHALF = 128
PEER_TOPK = 16
LN_EPS = 1e-5
ALPHA = (2 * DEPTH) ** 0.25
OFF_Q = D_POOL
OFF_K = OFF_Q + D_RET
OFF_V = OFF_K + D_RET
OFF_G = OFF_V + D_RET
OFF_BG = OFF_G + D_RET
D_IN = OFF_BG + 2 * D_MODEL
LOG_G = tuple(math.log(1.0 - 2.0 ** (-5.0 - h)) for h in range(N_RET_HEADS))
LANES = 128
SUBLANES = 8
KEY_TILES = PEER_NKEYS // SUBLANES
PACKED_ROWS = 16
ACT_PIECES = 2
ACT_LAG = 1
VMEM_LIMIT = 60 * 1024 * 1024
ADA_COLS = 1536
PEER_CHUNK_KEYS = 16
PROMPT_TILES = (256, 512)
SAMPLE_TILES = (32, 256)
NEG_INF = float("-inf")


def _ln(x):
    mu = jnp.mean(x, axis=-1, keepdims=True)
    xc = x - mu
    var = jnp.mean(xc * xc, axis=-1, keepdims=True)
    return xc * lax.rsqrt(var + LN_EPS)


def _gelu_tanh(x):
    a0 = -2.0 * math.sqrt(2.0 / math.pi) * math.log2(math.e)
    return x / (1.0 + jnp.exp2(x * (a0 + (a0 * 0.044715) * (x * x))))


def _params(*sem):
    return pltpu.CompilerParams(dimension_semantics=sem, vmem_limit_bytes=VMEM_LIMIT)


def _ada_kernel(c_ref, w_ref, b_ref, o_ref):
    c = c_ref[...]
    a = (c * jax.nn.sigmoid(c)).astype(BF16)
    o_ref[...] = jnp.dot(a, w_ref[...].astype(BF16), preferred_element_type=F32) + b_ref[...]


def _ada(c_pad, w_ada, b_ada):
    rows = c_pad.shape[0]
    n_out = w_ada.shape[-1]
    ct = ADA_COLS
    return pl.pallas_call(
        _ada_kernel,
        grid=(DEPTH, n_out // ct),
        in_specs=[
            pl.BlockSpec((rows, D_MODEL), lambda l, j: (0, 0)),
            pl.BlockSpec((None, D_MODEL, ct), lambda l, j: (l, 0, j)),
            pl.BlockSpec((None, 1, ct), lambda l, j: (l, 0, j)),
        ],
        out_specs=pl.BlockSpec((None, rows, ct), lambda l, j: (l, 0, j)),
        out_shape=jax.ShapeDtypeStruct((DEPTH, rows, n_out), F32),
        compiler_params=_params("arbitrary", "arbitrary"),
        name="ada",
    )(c_pad, w_ada, b_ada.reshape(DEPTH, 1, n_out))


def _mod_spec(per_token, tt, tiles_per_seq):
    if per_token:
        return pl.BlockSpec((1, tt, D_MODEL), lambda i, *_: (0, i, 0))
    return pl.BlockSpec((1, 1, D_MODEL), lambda i, *_: (i // tiles_per_seq, 0, 0))


def _mix_kernel(x_ref, sc_ref, sh_ref, g1_ref, cos_ref, sin_ref, hist_ref, s0_ref, win_ref,
                wgrp_ref, pscale_ref, wbp_ref, wbr_ref, wout_ref, lng_ref, lnb_ref,
                x1_ref, sfin_ref, hout_ref,
                proj_ref, state_scr, ext_scr, decay_scr, xi_scr, zeta_scr, retg_scr, pool_scr,
                *, tt, pos0, n_tiles):
    j = pl.program_id(1)

    h_in = (_ln(x_ref[0]) * (1.0 + sc_ref[0]) + sh_ref[0]).astype(BF16)
    proj_ref[...] = jnp.dot(h_in, win_ref[...], preferred_element_type=F32)

    @pl.when(j == 0)
    def _init():
        state_scr[...] = s0_ref[0]
        ext_scr[0:HIST_ROWS, :] = hist_ref[0]
        row = lax.broadcasted_iota(jnp.int32, (tt, tt), 0)
        col = lax.broadcasted_iota(jnp.int32, (tt, tt), 1)
        causal = row >= col
        diff = jnp.where(causal, (row - col).astype(F32), 0.0)
        rowl = lax.broadcasted_iota(jnp.int32, (tt, LANES), 0).astype(F32)
        for h in range(N_RET_HEADS):
            decay_scr[h] = jnp.where(causal, jnp.exp(LOG_G[h] * diff), 0.0)
            xi_scr[h] = jnp.exp(LOG_G[h] * (rowl + 1.0))
            zeta_scr[h] = jnp.exp(LOG_G[h] * (tt - 1.0 - rowl))

    p = proj_ref[:,0:D_POOL]
    ext_scr[HIST_ROWS:HIST_ROWS + tt, :] = p
    pos = pos0 + j * tt + lax.broadcasted_iota(jnp.int32, (tt, POOL_GROUP), 0)
    for g, w in enumerate(POOL_WINDOWS):
        cs = slice(g * POOL_GROUP, (g + 1) * POOL_GROUP)
        ws = ext_scr[HIST_ROWS:HIST_ROWS + tt, cs]
        for d in range(1, w):
            ws = ws + ext_scr[HIST_ROWS - d:HIST_ROWS - d + tt, cs]
        cnt = jnp.minimum(pos + 1, w).astype(F32)
        pooled = ws / cnt - p[:, cs]
        mixed = jnp.dot(pooled.astype(BF16), wgrp_ref[g], preferred_element_type=F32)
        pool_scr[:, cs] = (mixed * pscale_ref[:, cs]).astype(BF16)
    ext_scr[0:HIST_ROWS, :] = ext_scr[tt:tt + HIST_ROWS, :]

    cosv = cos_ref[...]
    sinv = sin_ref[...]
    for h in range(N_RET_HEADS):
        hs = slice(h * DK_RET, (h + 1) * DK_RET)
        q = proj_ref[:,OFF_Q + h * DK_RET:OFF_Q + (h + 1) * DK_RET]
        k = proj_ref[:,OFF_K + h * DK_RET:OFF_K + (h + 1) * DK_RET]
        v = proj_ref[:,OFF_V + h * DK_RET:OFF_V + (h + 1) * DK_RET].astype(BF16)
        g = proj_ref[:,OFF_G + h * DK_RET:OFF_G + (h + 1) * DK_RET]
        qr = q * cosv + pltpu.roll(q, DK_RET // 2, 1) * sinv
        kr = (k * cosv + pltpu.roll(k, DK_RET // 2, 1) * sinv) * RET_SCALE
        qb = qr.astype(BF16)
        scores = lax.dot_general(qb, kr.astype(BF16), (((1,), (1,)), ((), ())),
                                 preferred_element_type=F32) * decay_scr[h]
        intra = jnp.dot(scores.astype(BF16), v, preferred_element_type=F32)
        s_prev = state_scr[h]
        cross = jnp.dot(qb, s_prev.astype(BF16), preferred_element_type=F32) * xi_scr[h]
        kz = (kr * zeta_scr[h]).astype(BF16)
        kv = lax.dot_general(kz, v, (((0,), (0,)), ((), ())), preferred_element_type=F32)
        state_scr[h] = math.exp(LOG_G[h] * tt) * s_prev + kv
        retg_scr[:, hs] = (_ln(intra + cross) * (g * jax.nn.sigmoid(g))).astype(BF16)

    bgp = jax.nn.sigmoid(proj_ref[:,OFF_BG:OFF_BG + D_MODEL])
    bgr = jax.nn.sigmoid(proj_ref[:,OFF_BG + D_MODEL:OFF_BG + 2 * D_MODEL])
    pb = jnp.dot(pool_scr[...], wbp_ref[...], preferred_element_type=F32)
    rb = jnp.dot(retg_scr[...], wbr_ref[...], preferred_element_type=F32)
    merged = (bgp * pb + bgr * rb).astype(BF16)
    z = jnp.dot(merged, wout_ref[...], preferred_element_type=F32)
    x1_ref[0] = _ln(ALPHA * x_ref[0] + g1_ref[0] * z) * lng_ref[...] + lnb_ref[...]

    @pl.when(j == n_tiles - 1)
    def _fin():
        sfin_ref[0] = state_scr[...]
        hout_ref[0] = ext_scr[0:HIST_ROWS, :]


def _mix(x, sc1, sh1, g1, cos2, sin2, hist16, s0, wts, layer, tt, pos0):
    bsz, seq, _ = x.shape
    n_tiles = seq // tt
    w_in_b, wgrp_b, pscale, wbp_b, wbr_b, wout_b, ln1_g, ln1_b = wts
    const2 = lambda b, j: (layer, 0, 0)
    once = pl.Buffered(1)
    seq_row = pl.BlockSpec((1, 1, D_MODEL), lambda b, j: (b, 0, 0))
    return pl.pallas_call(
        functools.partial(_mix_kernel, tt=tt, pos0=pos0, n_tiles=n_tiles),
        grid=(bsz, n_tiles),
        in_specs=[
            pl.BlockSpec((1, tt, D_MODEL), lambda b, j: (b, j, 0)),
            seq_row, seq_row, seq_row,
            pl.BlockSpec((tt, LANES), lambda b, j: (j, 0)),
            pl.BlockSpec((tt, LANES), lambda b, j: (j, 0)),
            pl.BlockSpec((1, HIST_ROWS, D_POOL), lambda b, j: (b, 0, 0)),
            pl.BlockSpec((1, N_RET_HEADS, DK_RET, DK_RET), lambda b, j: (b, 0, 0, 0)),
            pl.BlockSpec((None, D_MODEL, D_IN), const2, pipeline_mode=once),
            pl.BlockSpec((None, len(POOL_WINDOWS), POOL_GROUP, POOL_GROUP),
                         lambda b, j: (layer, 0, 0, 0)),
            pl.BlockSpec((None, 1, D_POOL), const2),
            pl.BlockSpec((None, D_POOL, D_MODEL), const2, pipeline_mode=once),
            pl.BlockSpec((None, D_RET, D_MODEL), const2, pipeline_mode=once),
            pl.BlockSpec((None, D_MODEL, D_MODEL), const2, pipeline_mode=once),
            pl.BlockSpec((None, 1, D_MODEL), const2),
            pl.BlockSpec((None, 1, D_MODEL), const2),
        ],
        out_specs=[
            pl.BlockSpec((1, tt, D_MODEL), lambda b, j: (b, j, 0)),
            pl.BlockSpec((1, N_RET_HEADS, DK_RET, DK_RET), lambda b, j: (b, 0, 0, 0)),
            pl.BlockSpec((1, HIST_ROWS, D_POOL), lambda b, j: (b, 0, 0)),
        ],
        out_shape=[
            jax.ShapeDtypeStruct((bsz, seq, D_MODEL), F32),
            jax.ShapeDtypeStruct((bsz, N_RET_HEADS, DK_RET, DK_RET), F32),
            jax.ShapeDtypeStruct((bsz, HIST_ROWS, D_POOL), F32),
        ],
        scratch_shapes=[
            pltpu.VMEM((tt, D_IN), F32),
            pltpu.VMEM((N_RET_HEADS, DK_RET, DK_RET), F32),
            pltpu.VMEM((HIST_ROWS + tt, D_POOL), F32),
            pltpu.VMEM((N_RET_HEADS, tt, tt), F32),
            pltpu.VMEM((N_RET_HEADS, tt, LANES), F32),
            pltpu.VMEM((N_RET_HEADS, tt, LANES), F32),
            pltpu.VMEM((tt, D_RET), BF16),
            pltpu.VMEM((tt, D_POOL), BF16),
        ],
        compiler_params=_params("arbitrary", "arbitrary"),
        name="mix",
    )(x, sc1, sh1, g1, cos2, sin2, hist16, s0, w_in_b, wgrp_b, pscale, wbp_b, wbr_b, wout_b,
      ln1_g, ln1_b)


def _sort_tiles_desc(tiles):
    a = list(tiles)
    n = len(a)
    k = 2
    while k <= n:
        j = k // 2
        while j >= 1:
            for i in range(n):
                l = i ^ j
                if l > i:
                    hi = jnp.maximum(a[i], a[l])
                    lo = jnp.minimum(a[i], a[l])
                    a[i], a[l] = (hi, lo) if (i & k) == 0 else (lo, hi)
            j //= 2
        k *= 2
    return a


def _merge_top16(lists, extra=None):
    cur = list(lists)
    rows = []
    for r in range(PEER_TOPK):
        m = jnp.max(cur[0], axis=0, keepdims=True)
        if extra is not None:
            m = jnp.maximum(m, jnp.max(extra, axis=0, keepdims=True))
        rows.append(m)
        if r + 1 < PEER_TOPK:
            hit = cur[0] == m
            cur = [jnp.where(hit, cur[j + 1], cur[j]) for j in range(PEER_TOPK - 1 - r)]
            if extra is not None:
                extra = jnp.where(extra == m, NEG_INF, extra)
    return rows


def _top16_rows(s):
    tiles = [s[k * SUBLANES:(k + 1) * SUBLANES, :] for k in range(KEY_TILES)]
    return _merge_top16(_sort_tiles_desc(tiles))


def _stack_rows(rows, tt):
    ridx = lax.broadcasted_iota(jnp.int32, (len(rows), tt), 0)
    out = jnp.zeros((len(rows), tt), F32)
    for i, r in enumerate(rows):
        out = jnp.where(ridx == i, r, out)
    return out


def _activations(u_rows, h2_scr, s_dst, rows, n_lg):
    s_val = jnp.dot(u_rows, h2_scr[...], preferred_element_type=F32)
    for lg in range(n_lg):
        s_dst[lg, rows, :] = s_val[:, lg * LANES:(lg + 1) * LANES]


def _peer_kernel(x1_ref, sc_ref, sh_ref, g2_ref, wqt_ref, keys_ref, u_ref, vt_ref,
                 lng_ref, lnb_ref, o_ref,
                 h2_scr, c_scr, e1_scr, s2_scr, e2_scr, cb_scr, eb_scr, s_a, s_b, ga_scr, acc_scr,
                 *, tt, na, n_chunks):
    e = pl.program_id(1)
    n_lg = tt // LANES

    @pl.when(e == 0)
    def _prep():
        h2 = _ln(x1_ref[...]) * (1.0 + sc_ref[0]) + sh_ref[0]
        h2_scr[...] = h2.T.astype(BF16)
        qt = jnp.dot(wqt_ref[...], h2_scr[...], preferred_element_type=F32)
        qt_scr = ga_scr
        qt_scr[...] = qt.astype(BF16)

        for h in range(PEER_HEADS):
            base = h * PEER_DQ
            c_scr[h] = jnp.dot(keys_ref[0], qt_scr[base:base + PEER_DHALF, :],
                               preferred_element_type=F32)
            e1_scr[h] = jnp.dot(keys_ref[1], qt_scr[base + PEER_DHALF:base + PEER_DQ, :],
                                preferred_element_type=F32)

        def head(h, carry):
            s1 = c_scr[h]
            s2 = e1_scr[h]
            r1 = _top16_rows(s1)
            r2 = _top16_rows(s2)
            v2_lo = _stack_rows(r2[:SUBLANES], tt)
            v2_hi = _stack_rows(r2[SUBLANES:], tt)
            cand = [r1[i] + v2_lo for i in range(PEER_TOPK)]
            singles = r1[0] + v2_hi
            tau = _merge_top16(cand, singles)[PEER_TOPK - 1]
            top = r1[0] + r2[0]
            zsum = jnp.zeros((SUBLANES, tt), F32)
            for c in cand + [singles]:
                zsum = zsum + jnp.where(c >= tau, jnp.exp(c - top), 0.0)
            z = jnp.sum(zsum, axis=0, keepdims=True)
            c_scr[h] = tau - s1
            e1_scr[h] = jnp.exp(s1 - r1[0])
            e2 = jnp.exp(s2 - r2[0]) / z
            for lg in range(n_lg):
                ls = slice(lg * LANES, (lg + 1) * LANES)
                s2_scr[h, lg] = s2[:, ls].reshape(KEY_TILES, SUBLANES, LANES)
                e2_scr[h, lg] = e2[:, ls].reshape(KEY_TILES, SUBLANES, LANES)
            return carry

        lax.fori_loop(0, PEER_HEADS, head, 0)

        acc_scr[...] = jnp.zeros_like(acc_scr)
        _activations(u_ref[...], h2_scr, s_a, slice(None), n_lg)

    def step(s_w, s_r):
        a0 = pl.multiple_of((e - 1) * na, na)
        for h in range(PEER_HEADS):
            cslab = c_scr[h, pl.ds(a0, na), :]
            eslab = e1_scr[h, pl.ds(a0, na), :]
            for r in range(na):
                cb = jnp.broadcast_to(cslab[r:r + 1, :], (SUBLANES, tt))
                eb = jnp.broadcast_to(eslab[r:r + 1, :], (SUBLANES, tt))
                for lg in range(n_lg):
                    cb_scr[h, r, lg] = cb[:, lg * LANES:(lg + 1) * LANES]
                    eb_scr[h, r, lg] = eb[:, lg * LANES:(lg + 1) * LANES]

        halves = [tuple(range(lo, min(lo + 2, n_lg))) for lo in range(0, n_lg, 2)]
        groups = [(half, ai, lg) for half in halves for ai in range(na) for lg in half]
        act_at = [ACT_LAG + p * (len(groups) // ACT_PIECES) for p in range(ACT_PIECES)]
        rows_per = na * PEER_NKEYS // ACT_PIECES
        ga = None
        for gi, (half, ai, lg) in enumerate(groups):
            if s_w is not None and gi in act_at:
                p = act_at.index(gi)
                start = p * rows_per
                if ga is not None:
                    probe = jnp.max(ga[0:SUBLANES, :])
                    other = (start + rows_per) % (na * PEER_NKEYS)
                    start = jnp.where(probe != probe, other, start)
                start = pl.multiple_of(start, PACKED_ROWS)
                _activations(u_ref[pl.ds(start, rows_per), :], h2_scr, s_w,
                             pl.ds(start, rows_per), n_lg)
            gate = jnp.zeros((KEY_TILES, SUBLANES, LANES), F32)
            for h in range(PEER_HEADS):
                hit = s2_scr[h, lg] >= cb_scr[h, ai, lg][None]
                gate = gate + jnp.where(hit, e2_scr[h, lg], 0.0) * eb_scr[h, ai, lg][None]
            act = _gelu_tanh(s_r[lg, ai * PEER_NKEYS:(ai + 1) * PEER_NKEYS, :])
            ga = gate.reshape(PEER_NKEYS, LANES) * act
            ga_scr[ai * PEER_NKEYS:(ai + 1) * PEER_NKEYS, lg * LANES:(lg + 1) * LANES] = ga.astype(BF16)
            if ai == na - 1 and lg == half[-1]:
                hs = slice(half[0] * LANES, (half[-1] + 1) * LANES)
                acc_scr[:, hs] += jnp.dot(vt_ref[...], ga_scr[:, hs], preferred_element_type=F32)

    assert n_chunks % 2 == 0

    @pl.when(e % 2 == 1)
    def _odd():
        step(s_b, s_a)

    @pl.when((e > 0) & (e % 2 == 0) & (e < n_chunks))
    def _even():
        step(s_a, s_b)

    @pl.when(e == n_chunks)
    def _last():
        step(None, s_b)
        y = acc_scr[...].T
        o_ref[...] = _ln(ALPHA * x1_ref[...] + g2_ref[0] * y) * lng_ref[...] + lnb_ref[...]


def _peer(x1_flat, sc, sh, g2, wts, layer, tt, per_token, tiles_per_seq):
    t = x1_flat.shape[0]
    wqt_b, keys_b, u_b, vt_b, ln2_g, ln2_b = wts
    na = PEER_CHUNK_KEYS
    ec = na * PEER_NKEYS
    assert ec == PEER_HEADS * PEER_DQ
    n_chunks = PEER_N // ec
    n_lg = tt // LANES
    mod = _mod_spec(per_token, tt, tiles_per_seq)
    return pl.pallas_call(
        functools.partial(_peer_kernel, tt=tt, na=na, n_chunks=n_chunks),
        grid=(t // tt, n_chunks + 1),
        in_specs=[
            pl.BlockSpec((tt, D_MODEL), lambda i, e: (i, 0)),
            mod, mod, mod,
            pl.BlockSpec((None, PEER_HEADS * PEER_DQ, D_MODEL), lambda i, e: (layer, 0, 0)),
            pl.BlockSpec((None, 2, PEER_NKEYS, PEER_DHALF), lambda i, e: (layer, 0, 0, 0)),
            pl.BlockSpec((None, ec, D_MODEL),
                         lambda i, e: (layer, jnp.minimum(e, n_chunks - 1), 0)),
            pl.BlockSpec((None, D_MODEL, ec), lambda i, e: (layer, 0, jnp.maximum(e - 1, 0))),
            pl.BlockSpec((None, 1, D_MODEL), lambda i, e: (layer, 0, 0)),
            pl.BlockSpec((None, 1, D_MODEL), lambda i, e: (layer, 0, 0)),
        ],
        out_specs=pl.BlockSpec((tt, D_MODEL), lambda i, e: (i, 0)),
        out_shape=jax.ShapeDtypeStruct((t, D_MODEL), F32),
        scratch_shapes=[
            pltpu.VMEM((D_MODEL, tt), BF16),
            pltpu.VMEM((PEER_HEADS, PEER_NKEYS, tt), F32),
            pltpu.VMEM((PEER_HEADS, PEER_NKEYS, tt), F32),
            pltpu.VMEM((PEER_HEADS, n_lg, KEY_TILES, SUBLANES, LANES), F32),
            pltpu.VMEM((PEER_HEADS, n_lg, KEY_TILES, SUBLANES, LANES), F32),
            pltpu.VMEM((PEER_HEADS, na, n_lg, SUBLANES, LANES), F32),
            pltpu.VMEM((PEER_HEADS, na, n_lg, SUBLANES, LANES), F32),
            pltpu.VMEM((n_lg, ec, LANES), F32),
            pltpu.VMEM((n_lg, ec, LANES), F32),
            pltpu.VMEM((ec, tt), BF16),
            pltpu.VMEM((D_MODEL, tt), F32),
        ],
        compiler_params=_params("arbitrary", "arbitrary"),
        name="peer",
    )(x1_flat, sc, sh, g2, wqt_b, keys_b, u_b, vt_b, ln2_g, ln2_b)


def _rope_tables(pos0, seq):
    half = DK_RET // 2
    inv = ROPE_BASE ** (-jnp.arange(half, dtype=F32) / half)
    ang = (pos0 + jnp.arange(seq)).astype(F32)[:, None] * inv[None, :]
    cos = jnp.cos(ang)
    sin = jnp.sin(ang)
    return jnp.concatenate([cos, cos], axis=-1), jnp.concatenate([-sin, sin], axis=-1)


def _trunk(x, mods, pool_hist, ret_state, pos0, wts, tiles):
    bsz, seq, _ = x.shape
    tt_mix, tt_peer = tiles
    per_token = seq < tt_peer
    cos2, sin2 = _rope_tables(pos0, seq)
    hist16 = jnp.pad(pool_hist, ((0, 0), (0, 0), (HIST_ROWS - POOL_HIST, 0), (0, 0)))
    new_hist, new_state = [], []
    for l in range(DEPTH):
        chunks = [mods[l, :, k * D_MODEL:(k + 1) * D_MODEL] for k in range(6)]
        seq_rows = [c[:, None, :] for c in chunks]
        if per_token:
            tok_rows = [jnp.repeat(c, seq, axis=0)[None] for c in chunks]
        else:
            tok_rows = seq_rows
        _, _, _, sh2, sc2, g2 = tok_rows
        sh1, sc1, g1 = seq_rows[:3]
        x1, s_l, hist_l = _mix(x, sc1, sh1, g1, cos2, sin2, hist16[l], ret_state[l],
                               (wts["w_in"], wts["w_grp"], wts["pool_scale"], wts["w_bp"],
                                wts["w_br"], wts["w_out"], wts["ln1_g"], wts["ln1_b"]),
                               l, tt_mix, pos0)
        x2 = _peer(x1.reshape(bsz * seq, D_MODEL), sc2, sh2, g2,
                   (wts["w_qt"], wts["keys"], wts["u"], wts["vt"], wts["ln2_g"], wts["ln2_b"]),
                   l, tt_peer, per_token, seq // tt_peer if not per_token else 1)
        x = x2.reshape(bsz, seq, D_MODEL)
        new_hist.append(hist_l[:, HIST_ROWS - POOL_HIST:, :])
        new_state.append(s_l)
    return x, jnp.stack(new_hist), jnp.stack(new_state)


def kernel(x_prompt, x_sample, cache_pool, state_ret, c_prompt, c_sample, w_ada, b_ada, w_in,
           w_pool_grp, pool_scale, w_branch_pool, w_branch_ret, w_out, ln1_g, ln1_b, w_peer_q,
           peer_sub_keys, peer_u, peer_v, ln2_g, ln2_b):
    n_prompt = x_prompt.shape[0]
    n_sample = x_sample.shape[0]
    wts = {
        "w_in": w_in.astype(BF16),
        "w_grp": w_pool_grp.astype(BF16),
        "pool_scale": pool_scale[:, None, :],
        "w_bp": w_branch_pool.astype(BF16),
        "w_br": w_branch_ret.astype(BF16),
        "w_out": w_out.astype(BF16),
        "ln1_g": ln1_g[:, None, :],
        "ln1_b": ln1_b[:, None, :],
        "w_qt": jnp.swapaxes(w_peer_q, 1, 2).astype(BF16),
        "keys": peer_sub_keys.astype(BF16),
        "u": peer_u.astype(BF16),
        "vt": jnp.swapaxes(peer_v, 1, 2).astype(BF16),
        "ln2_g": ln2_g[:, None, :],
        "ln2_b": ln2_b[:, None, :],
    }
    n_seq = n_prompt + n_sample
    c_pad = jnp.pad(jnp.concatenate([c_prompt, c_sample], axis=0), ((0, 16 - n_seq), (0, 0)))
    mods = _ada(c_pad, w_ada, b_ada)
    zero_hist = jnp.zeros((DEPTH, n_prompt, POOL_HIST, D_POOL), x_prompt.dtype)
    zero_state = jnp.zeros((DEPTH, n_prompt, N_RET_HEADS, DK_RET, DK_RET), F32)
    y_p, pool_p, ret_p = _trunk(x_prompt, mods[:, :n_prompt], zero_hist, zero_state, 0, wts,
                                PROMPT_TILES)
    y_s, pool_s, ret_s = _trunk(x_sample, mods[:, n_prompt:n_seq], cache_pool, state_ret,
                                PAST_LEN, wts, SAMPLE_TILES)
    return (y_p, y_s, pool_p, ret_p.astype(state_ret.dtype), pool_s, ret_s.astype(state_ret.dtype))
```

```python
import functools
import math

import jax
import jax.numpy as jnp
from jax import lax
from jax.experimental import pallas as pl
from jax.experimental.pallas import tpu as pltpu

F32 = jnp.float32
BF16 = jnp.bfloat16

D_MODEL = 1024
DEPTH = 2
PAST_LEN = 2048
POOL_WINDOWS = (2, 4, 8, 16)
D_POOL = 512
POOL_GROUP = 128
POOL_HIST = 15
HIST_ROWS = 16
N_RET_HEADS = 8
DK_RET = 128
D_RET = N_RET_HEADS * DK_RET
RET_SCALE = DK_RET ** -0.5
ROPE_BASE = 10000.0
PEER_HEADS = 8
PEER_NKEYS = 128
PEER_N = PEER_NKEYS * PEER_NKEYS
PEER_DQ = 256
PEER_DHALF = 128
PEER_TOPK = 16
LN_EPS = 1e-5
ALPHA = (2 * DEPTH) ** 0.25
OFF_Q = D_POOL
OFF_K = OFF_Q + D_RET
OFF_V = OFF_K + D_RET
OFF_G = OFF_V + D_RET
OFF_BG = OFF_G + D_RET
D_IN = OFF_BG + 2 * D_MODEL
LOG_G = tuple(math.log(1.0 - 2.0 ** (-5.0 - h)) for h in range(N_RET_HEADS))
LANES = 128
SUBLANES = 8
KEY_TILES = PEER_NKEYS // SUBLANES
PACKED_ROWS = 16
ACT_PIECES = 2
ACT_LAG = 1
VMEM_LIMIT = 60 * 1024 * 1024
ADA_COLS = 1536
PEER_CHUNK_KEYS = 16
PROMPT_TILES = (256, 512)
SAMPLE_TILES = (32, 256)
NEG_INF = float("-inf")


def _ln(x):
    mu = jnp.mean(x, axis=-1, keepdims=True)
    xc = x - mu
    var = jnp.mean(xc * xc, axis=-1, keepdims=True)
    return xc * lax.rsqrt(var + LN_EPS)


def _gelu_tanh(x):
    a0 = -2.0 * math.sqrt(2.0 / math.pi) * math.log2(math.e)
    return x / (1.0 + jnp.exp2(x * (a0 + (a0 * 0.044715) * (x * x))))


def _params(*sem):
    return pltpu.CompilerParams(dimension_semantics=sem, vmem_limit_bytes=VMEM_LIMIT)


def _ada_kernel(c_ref, w_ref, b_ref, o_ref):
    c = c_ref[...]
    a = (c * jax.nn.sigmoid(c)).astype(BF16)
    o_ref[...] = jnp.dot(a, w_ref[...].astype(BF16), preferred_element_type=F32) + b_ref[...]


def _ada(c_pad, w_ada, b_ada):
    rows = c_pad.shape[0]
    n_out = w_ada.shape[-1]
    ct = ADA_COLS
    return pl.pallas_call(
        _ada_kernel,
        grid=(DEPTH, n_out // ct),
        in_specs=[
            pl.BlockSpec((rows, D_MODEL), lambda l, j: (0, 0)),
            pl.BlockSpec((None, D_MODEL, ct), lambda l, j: (l, 0, j)),
            pl.BlockSpec((None, 1, ct), lambda l, j: (l, 0, j)),
        ],
        out_specs=pl.BlockSpec((None, rows, ct), lambda l, j: (l, 0, j)),
        out_shape=jax.ShapeDtypeStruct((DEPTH, rows, n_out), F32),
        compiler_params=_params("arbitrary", "arbitrary"),
        name="ada",
    )(c_pad, w_ada, b_ada.reshape(DEPTH, 1, n_out))


def _mod_spec(per_token, tt, tiles_per_seq):
    if per_token:
        return pl.BlockSpec((1, tt, D_MODEL), lambda i, *_: (0, i, 0))
    return pl.BlockSpec((1, 1, D_MODEL), lambda i, *_: (i // tiles_per_seq, 0, 0))


def _mix_kernel(x_ref, sc_ref, sh_ref, g1_ref, cos_ref, sin_ref, hist_ref, s0_ref, win_ref,
                wgrp_ref, pscale_ref, wbp_ref, wbr_ref, wout_ref, lng_ref, lnb_ref,
                x1_ref, sfin_ref, hout_ref,
                proj_ref, state_scr, ext_scr, decay_scr, xi_scr, zeta_scr, retg_scr, pool_scr,
                *, tt, pos0, n_tiles):
    j = pl.program_id(1)

    h_in = (_ln(x_ref[0]) * (1.0 + sc_ref[0]) + sh_ref[0]).astype(BF16)
    proj_ref[...] = jnp.dot(h_in, win_ref[...], preferred_element_type=F32)

    @pl.when(j == 0)
    def _init():
        state_scr[...] = s0_ref[0]
        ext_scr[0:HIST_ROWS, :] = hist_ref[0]
        row = lax.broadcasted_iota(jnp.int32, (tt, tt), 0)
        col = lax.broadcasted_iota(jnp.int32, (tt, tt), 1)
        causal = row >= col
        diff = jnp.where(causal, (row - col).astype(F32), 0.0)
        rowl = lax.broadcasted_iota(jnp.int32, (tt, LANES), 0).astype(F32)
        for h in range(N_RET_HEADS):
            decay_scr[h] = jnp.where(causal, jnp.exp(LOG_G[h] * diff), 0.0)
            xi_scr[h] = jnp.exp(LOG_G[h] * (rowl + 1.0))
            zeta_scr[h] = jnp.exp(LOG_G[h] * (tt - 1.0 - rowl))

    p = proj_ref[:,0:D_POOL]
    ext_scr[HIST_ROWS:HIST_ROWS + tt, :] = p
    pos = pos0 + j * tt + lax.broadcasted_iota(jnp.int32, (tt, POOL_GROUP), 0)
    for g, w in enumerate(POOL_WINDOWS):
        cs = slice(g * POOL_GROUP, (g + 1) * POOL_GROUP)
        ws = ext_scr[HIST_ROWS:HIST_ROWS + tt, cs]
        for d in range(1, w):
            ws = ws + ext_scr[HIST_ROWS - d:HIST_ROWS - d + tt, cs]
        cnt = jnp.minimum(pos + 1, w).astype(F32)
        pooled = ws / cnt - p[:, cs]
        mixed = jnp.dot(pooled.astype(BF16), wgrp_ref[g], preferred_element_type=F32)
        pool_scr[:, cs] = (mixed * pscale_ref[:, cs]).astype(BF16)
    ext_scr[0:HIST_ROWS, :] = ext_scr[tt:tt + HIST_ROWS, :]

    cosv = cos_ref[...]
    sinv = sin_ref[...]
    for h in range(N_RET_HEADS):
        hs = slice(h * DK_RET, (h + 1) * DK_RET)
        q = proj_ref[:,OFF_Q + h * DK_RET:OFF_Q + (h + 1) * DK_RET]
        k = proj_ref[:,OFF_K + h * DK_RET:OFF_K + (h + 1) * DK_RET]
        v = proj_ref[:,OFF_V + h * DK_RET:OFF_V + (h + 1) * DK_RET].astype(BF16)
        g = proj_ref[:,OFF_G + h * DK_RET:OFF_G + (h + 1) * DK_RET]
        qr = q * cosv + pltpu.roll(q, DK_RET // 2, 1) * sinv
        kr = (k * cosv + pltpu.roll(k, DK_RET // 2, 1) * sinv) * RET_SCALE
        qb = qr.astype(BF16)
        scores = lax.dot_general(qb, kr.astype(BF16), (((1,), (1,)), ((), ())),
                                 preferred_element_type=F32) * decay_scr[h]
        intra = jnp.dot(scores.astype(BF16), v, preferred_element_type=F32)
        s_prev = state_scr[h]
        cross = jnp.dot(qb, s_prev.astype(BF16), preferred_element_type=F32) * xi_scr[h]
        kz = (kr * zeta_scr[h]).astype(BF16)
        kv = lax.dot_general(kz, v, (((0,), (0,)), ((), ())), preferred_element_type=F32)
        state_scr[h] = math.exp(LOG_G[h] * tt) * s_prev + kv
        retg_scr[:, hs] = (_ln(intra + cross) * (g * jax.nn.sigmoid(g))).astype(BF16)

    bgp = jax.nn.sigmoid(proj_ref[:,OFF_BG:OFF_BG + D_MODEL])
    bgr = jax.nn.sigmoid(proj_ref[:,OFF_BG + D_MODEL:OFF_BG + 2 * D_MODEL])
    pb = jnp.dot(pool_scr[...], wbp_ref[...], preferred_element_type=F32)
    rb = jnp.dot(retg_scr[...], wbr_ref[...], preferred_element_type=F32)
    merged = (bgp * pb + bgr * rb).astype(BF16)
    z = jnp.dot(merged, wout_ref[...], preferred_element_type=F32)
    x1_ref[0] = _ln(ALPHA * x_ref[0] + g1_ref[0] * z) * lng_ref[...] + lnb_ref[...]

    @pl.when(j == n_tiles - 1)
    def _fin():
        sfin_ref[0] = state_scr[...]
        hout_ref[0] = ext_scr[0:HIST_ROWS, :]


def _mix(x, sc1, sh1, g1, cos2, sin2, hist16, s0, wts, layer, tt, pos0):
    bsz, seq, _ = x.shape
    n_tiles = seq // tt
    w_in_b, wgrp_b, pscale, wbp_b, wbr_b, wout_b, ln1_g, ln1_b = wts
    const2 = lambda b, j: (layer, 0, 0)
    once = pl.Buffered(1)
    seq_row = pl.BlockSpec((1, 1, D_MODEL), lambda b, j: (b, 0, 0))
    return pl.pallas_call(
        functools.partial(_mix_kernel, tt=tt, pos0=pos0, n_tiles=n_tiles),
        grid=(bsz, n_tiles),
        in_specs=[
            pl.BlockSpec((1, tt, D_MODEL), lambda b, j: (b, j, 0)),
            seq_row, seq_row, seq_row,
            pl.BlockSpec((tt, LANES), lambda b, j: (j, 0)),
            pl.BlockSpec((tt, LANES), lambda b, j: (j, 0)),
            pl.BlockSpec((1, HIST_ROWS, D_POOL), lambda b, j: (b, 0, 0)),
            pl.BlockSpec((1, N_RET_HEADS, DK_RET, DK_RET), lambda b, j: (b, 0, 0, 0)),
            pl.BlockSpec((None, D_MODEL, D_IN), const2, pipeline_mode=once),
            pl.BlockSpec((None, len(POOL_WINDOWS), POOL_GROUP, POOL_GROUP),
                         lambda b, j: (layer, 0, 0, 0)),
            pl.BlockSpec((None, 1, D_POOL), const2),
            pl.BlockSpec((None, D_POOL, D_MODEL), const2, pipeline_mode=once),
            pl.BlockSpec((None, D_RET, D_MODEL), const2, pipeline_mode=once),
            pl.BlockSpec((None, D_MODEL, D_MODEL), const2, pipeline_mode=once),
            pl.BlockSpec((None, 1, D_MODEL), const2),
            pl.BlockSpec((None, 1, D_MODEL), const2),
        ],
        out_specs=[
            pl.BlockSpec((1, tt, D_MODEL), lambda b, j: (b, j, 0)),
            pl.BlockSpec((1, N_RET_HEADS, DK_RET, DK_RET), lambda b, j: (b, 0, 0, 0)),
            pl.BlockSpec((1, HIST_ROWS, D_POOL), lambda b, j: (b, 0, 0)),
        ],
        out_shape=[
            jax.ShapeDtypeStruct((bsz, seq, D_MODEL), F32),
            jax.ShapeDtypeStruct((bsz, N_RET_HEADS, DK_RET, DK_RET), F32),
            jax.ShapeDtypeStruct((bsz, HIST_ROWS, D_POOL), F32),
        ],
        scratch_shapes=[
            pltpu.VMEM((tt, D_IN), F32),
            pltpu.VMEM((N_RET_HEADS, DK_RET, DK_RET), F32),
            pltpu.VMEM((HIST_ROWS + tt, D_POOL), F32),
            pltpu.VMEM((N_RET_HEADS, tt, tt), F32),
            pltpu.VMEM((N_RET_HEADS, tt, LANES), F32),
            pltpu.VMEM((N_RET_HEADS, tt, LANES), F32),
            pltpu.VMEM((tt, D_RET), BF16),
            pltpu.VMEM((tt, D_POOL), BF16),
        ],
        compiler_params=_params("arbitrary", "arbitrary"),
        name="mix",
    )(x, sc1, sh1, g1, cos2, sin2, hist16, s0, w_in_b, wgrp_b, pscale, wbp_b, wbr_b, wout_b,
      ln1_g, ln1_b)


def _sort_tiles_desc(tiles):
    a = list(tiles)
    n = len(a)
    k = 2
    while k <= n:
        j = k // 2
        while j >= 1:
            for i in range(n):
                l = i ^ j
                if l > i:
                    hi = jnp.maximum(a[i], a[l])
                    lo = jnp.minimum(a[i], a[l])
                    a[i], a[l] = (hi, lo) if (i & k) == 0 else (lo, hi)
            j //= 2
        k *= 2
    return a


def _merge_top16(lists, extra=None):
    cur = list(lists)
    rows = []
    for r in range(PEER_TOPK):
        m = jnp.max(cur[0], axis=0, keepdims=True)
        if extra is not None:
            m = jnp.maximum(m, jnp.max(extra, axis=0, keepdims=True))
        rows.append(m)
        if r + 1 < PEER_TOPK:
            hit = cur[0] == m
            cur = [jnp.where(hit, cur[j + 1], cur[j]) for j in range(PEER_TOPK - 1 - r)]
            if extra is not None:
                extra = jnp.where(extra == m, NEG_INF, extra)
    return rows


def _top16_rows(s):
    tiles = [s[k * SUBLANES:(k + 1) * SUBLANES, :] for k in range(KEY_TILES)]
    return _merge_top16(_sort_tiles_desc(tiles))


def _stack_rows(rows, tt):
    ridx = lax.broadcasted_iota(jnp.int32, (len(rows), tt), 0)
    out = jnp.zeros((len(rows), tt), F32)
    for i, r in enumerate(rows):
        out = jnp.where(ridx == i, r, out)
    return out


def _activations(u_rows, h2_scr, s_dst, rows, n_lg):
    s_val = jnp.dot(u_rows, h2_scr[...], preferred_element_type=F32)
    for lg in range(n_lg):
        s_dst[lg, rows, :] = s_val[:, lg * LANES:(lg + 1) * LANES]


def _peer_kernel(x1_ref, sc_ref, sh_ref, g2_ref, wqt_ref, keys_ref, u_ref, vt_ref,
                 lng_ref, lnb_ref, o_ref,
                 h2_scr, c_scr, e1_scr, s2_scr, e2_scr, cb_scr, eb_scr, s_a, s_b, ga_scr, acc_scr,
                 *, tt, na, n_chunks):
    e = pl.program_id(1)
    n_lg = tt // LANES

    @pl.when(e == 0)
    def _prep():
        h2 = _ln(x1_ref[...]) * (1.0 + sc_ref[0]) + sh_ref[0]
        h2_scr[...] = h2.T.astype(BF16)
        qt = jnp.dot(wqt_ref[...], h2_scr[...], preferred_element_type=F32)
        qt_scr = ga_scr
        qt_scr[...] = qt.astype(BF16)

        for h in range(PEER_HEADS):
            base = h * PEER_DQ
            c_scr[h] = jnp.dot(keys_ref[0], qt_scr[base:base + PEER_DHALF, :],
                               preferred_element_type=F32)
            e1_scr[h] = jnp.dot(keys_ref[1], qt_scr[base + PEER_DHALF:base + PEER_DQ, :],
                                preferred_element_type=F32)

        def head(h, carry):
            s1 = c_scr[h]
            s2 = e1_scr[h]
            r1 = _top16_rows(s1)
            r2 = _top16_rows(s2)
            v2_lo = _stack_rows(r2[:SUBLANES], tt)
            v2_hi = _stack_rows(r2[SUBLANES:], tt)
            cand = [r1[i] + v2_lo for i in range(PEER_TOPK)]
            singles = r1[0] + v2_hi
            tau = _merge_top16(cand, singles)[PEER_TOPK - 1]
            top = r1[0] + r2[0]
            zsum = jnp.zeros((SUBLANES, tt), F32)
            for c in cand + [singles]:
                zsum = zsum + jnp.where(c >= tau, jnp.exp(c - top), 0.0)
            z = jnp.sum(zsum, axis=0, keepdims=True)
            c_scr[h] = tau - s1
            e1_scr[h] = jnp.exp(s1 - r1[0])
            e2 = jnp.exp(s2 - r2[0]) / z
            for lg in range(n_lg):
                ls = slice(lg * LANES, (lg + 1) * LANES)
                s2_scr[h, lg] = s2[:, ls].reshape(KEY_TILES, SUBLANES, LANES)
                e2_scr[h, lg] = e2[:, ls].reshape(KEY_TILES, SUBLANES, LANES)
            return carry

        lax.fori_loop(0, PEER_HEADS, head, 0)

        acc_scr[...] = jnp.zeros_like(acc_scr)
        _activations(u_ref[...], h2_scr, s_a, slice(None), n_lg)

    def step(s_w, s_r):
        a0 = pl.multiple_of((e - 1) * na, na)
        for h in range(PEER_HEADS):
            cslab = c_scr[h, pl.ds(a0, na), :]
            eslab = e1_scr[h, pl.ds(a0, na), :]
            for r in range(na):
                cb = jnp.broadcast_to(cslab[r:r + 1, :], (SUBLANES, tt))
                eb = jnp.broadcast_to(eslab[r:r + 1, :], (SUBLANES, tt))
                for lg in range(n_lg):
                    cb_scr[h, r, lg] = cb[:, lg * LANES:(lg + 1) * LANES]
                    eb_scr[h, r, lg] = eb[:, lg * LANES:(lg + 1) * LANES]

        halves = [tuple(range(lo, min(lo + 2, n_lg))) for lo in range(0, n_lg, 2)]
        groups = [(half, ai, lg) for half in halves for lg in half for ai in range(na)]
        act_at = [ACT_LAG + p * (len(groups) // ACT_PIECES) for p in range(ACT_PIECES)]
        rows_per = na * PEER_NKEYS // ACT_PIECES
        ga = None
        for gi, (half, ai, lg) in enumerate(groups):
            if s_w is not None and gi in act_at:
                p = act_at.index(gi)
                start = p * rows_per
                if ga is not None:
                    probe = jnp.max(ga[0:SUBLANES, :])
                    other = (start + rows_per) % (na * PEER_NKEYS)
                    start = jnp.where(probe != probe, other, start)
                start = pl.multiple_of(start, PACKED_ROWS)
                _activations(u_ref[pl.ds(start, rows_per), :], h2_scr, s_w,
                             pl.ds(start, rows_per), n_lg)
            gate = jnp.zeros((KEY_TILES, SUBLANES, LANES), F32)
            for h in range(PEER_HEADS):
                hit = s2_scr[h, lg] >= cb_scr[h, ai, lg][None]
                gate = gate + jnp.where(hit, e2_scr[h, lg], 0.0) * eb_scr[h, ai, lg][None]
            act = _gelu_tanh(s_r[lg, ai * PEER_NKEYS:(ai + 1) * PEER_NKEYS, :])
            ga = gate.reshape(PEER_NKEYS, LANES) * act
            ga_scr[ai * PEER_NKEYS:(ai + 1) * PEER_NKEYS, lg * LANES:(lg + 1) * LANES] = ga.astype(BF16)
            if ai == na - 1 and lg == half[-1]:
                hs = slice(half[0] * LANES, (half[-1] + 1) * LANES)
                acc_scr[:, hs] += jnp.dot(vt_ref[...], ga_scr[:, hs], preferred_element_type=F32)

    assert n_chunks % 2 == 0

    @pl.when(e % 2 == 1)
    def _odd():
        step(s_b, s_a)

    @pl.when((e > 0) & (e % 2 == 0) & (e < n_chunks))
    def _even():
        step(s_a, s_b)

    @pl.when(e == n_chunks)
    def _last():
        step(None, s_b)
        y = acc_scr[...].T
        o_ref[...] = _ln(ALPHA * x1_ref[...] + g2_ref[0] * y) * lng_ref[...] + lnb_ref[...]


def _peer(x1_flat, sc, sh, g2, wts, layer, tt, per_token, tiles_per_seq):
    t = x1_flat.shape[0]
    wqt_b, keys_b, u_b, vt_b, ln2_g, ln2_b = wts
    na = PEER_CHUNK_KEYS
    ec = na * PEER_NKEYS
    assert ec == PEER_HEADS * PEER_DQ
    n_chunks = PEER_N // ec
    n_lg = tt // LANES
    mod = _mod_spec(per_token, tt, tiles_per_seq)
    return pl.pallas_call(
        functools.partial(_peer_kernel, tt=tt, na=na, n_chunks=n_chunks),
        grid=(t // tt, n_chunks + 1),
        in_specs=[
            pl.BlockSpec((tt, D_MODEL), lambda i, e: (i, 0)),
            mod, mod, mod,
            pl.BlockSpec((None, PEER_HEADS * PEER_DQ, D_MODEL), lambda i, e: (layer, 0, 0)),
            pl.BlockSpec((None, 2, PEER_NKEYS, PEER_DHALF), lambda i, e: (layer, 0, 0, 0)),
            pl.BlockSpec((None, ec, D_MODEL),
                         lambda i, e: (layer, jnp.minimum(e, n_chunks - 1), 0)),
            pl.BlockSpec((None, D_MODEL, ec), lambda i, e: (layer, 0, jnp.maximum(e - 1, 0))),
            pl.BlockSpec((None, 1, D_MODEL), lambda i, e: (layer, 0, 0)),
            pl.BlockSpec((None, 1, D_MODEL), lambda i, e: (layer, 0, 0)),
        ],
        out_specs=pl.BlockSpec((tt, D_MODEL), lambda i, e: (i, 0)),
        out_shape=jax.ShapeDtypeStruct((t, D_MODEL), F32),
        scratch_shapes=[
            pltpu.VMEM((D_MODEL, tt), BF16),
            pltpu.VMEM((PEER_HEADS, PEER_NKEYS, tt), F32),
            pltpu.VMEM((PEER_HEADS, PEER_NKEYS, tt), F32),
            pltpu.VMEM((PEER_HEADS, n_lg, KEY_TILES, SUBLANES, LANES), F32),
            pltpu.VMEM((PEER_HEADS, n_lg, KEY_TILES, SUBLANES, LANES), F32),
            pltpu.VMEM((PEER_HEADS, na, n_lg, SUBLANES, LANES), F32),
            pltpu.VMEM((PEER_HEADS, na, n_lg, SUBLANES, LANES), F32),
            pltpu.VMEM((n_lg, ec, LANES), F32),
            pltpu.VMEM((n_lg, ec, LANES), F32),
            pltpu.VMEM((ec, tt), BF16),
            pltpu.VMEM((D_MODEL, tt), F32),
        ],
        compiler_params=_params("arbitrary", "arbitrary"),
        name="peer",
    )(x1_flat, sc, sh, g2, wqt_b, keys_b, u_b, vt_b, ln2_g, ln2_b)


def _rope_tables(pos0, seq):
    half = DK_RET // 2
    inv = ROPE_BASE ** (-jnp.arange(half, dtype=F32) / half)
    ang = (pos0 + jnp.arange(seq)).astype(F32)[:, None] * inv[None, :]
    cos = jnp.cos(ang)
    sin = jnp.sin(ang)
    return jnp.concatenate([cos, cos], axis=-1), jnp.concatenate([-sin, sin], axis=-1)


def _trunk(x, mods, pool_hist, ret_state, pos0, wts, tiles):
    bsz, seq, _ = x.shape
    tt_mix, tt_peer = tiles
    per_token = seq < tt_peer
    cos2, sin2 = _rope_tables(pos0, seq)
    hist16 = jnp.pad(pool_hist, ((0, 0), (0, 0), (HIST_ROWS - POOL_HIST, 0), (0, 0)))
    new_hist, new_state = [], []
    for l in range(DEPTH):
        chunks = [mods[l, :, k * D_MODEL:(k + 1) * D_MODEL] for k in range(6)]
        seq_rows = [c[:, None, :] for c in chunks]
        if per_token:
            tok_rows = [jnp.repeat(c, seq, axis=0)[None] for c in chunks]
        else:
            tok_rows = seq_rows
        _, _, _, sh2, sc2, g2 = tok_rows
        sh1, sc1, g1 = seq_rows[:3]
        x1, s_l, hist_l = _mix(x, sc1, sh1, g1, cos2, sin2, hist16[l], ret_state[l],
                               (wts["w_in"], wts["w_grp"], wts["pool_scale"], wts["w_bp"],
                                wts["w_br"], wts["w_out"], wts["ln1_g"], wts["ln1_b"]),
                               l, tt_mix, pos0)
        x2 = _peer(x1.reshape(bsz * seq, D_MODEL), sc2, sh2, g2,
                   (wts["w_qt"], wts["keys"], wts["u"], wts["vt"], wts["ln2_g"], wts["ln2_b"]),
                   l, tt_peer, per_token, seq // tt_peer if not per_token else 1)
        x = x2.reshape(bsz, seq, D_MODEL)
        new_hist.append(hist_l[:, HIST_ROWS - POOL_HIST:, :])
        new_state.append(s_l)
    return x, jnp.stack(new_hist), jnp.stack(new_state)


def kernel(x_prompt, x_sample, cache_pool, state_ret, c_prompt, c_sample, w_ada, b_ada, w_in,
           w_pool_grp, pool_scale, w_branch_pool, w_branch_ret, w_out, ln1_g, ln1_b, w_peer_q,
           peer_sub_keys, peer_u, peer_v, ln2_g, ln2_b):
    n_prompt = x_prompt.shape[0]
    n_sample = x_sample.shape[0]
    wts = {
        "w_in": w_in.astype(BF16),
        "w_grp": w_pool_grp.astype(BF16),
        "pool_scale": pool_scale[:, None, :],
        "w_bp": w_branch_pool.astype(BF16),
        "w_br": w_branch_ret.astype(BF16),
        "w_out": w_out.astype(BF16),
        "ln1_g": ln1_g[:, None, :],
        "ln1_b": ln1_b[:, None, :],
        "w_qt": jnp.swapaxes(w_peer_q, 1, 2).astype(BF16),
        "keys": peer_sub_keys.astype(BF16),
        "u": peer_u.astype(BF16),
        "vt": jnp.swapaxes(peer_v, 1, 2).astype(BF16),
        "ln2_g": ln2_g[:, None, :],
        "ln2_b": ln2_b[:, None, :],
    }
    n_seq = n_prompt + n_sample
    c_pad = jnp.pad(jnp.concatenate([c_prompt, c_sample], axis=0), ((0, 16 - n_seq), (0, 0)))
    mods = _ada(c_pad, w_ada, b_ada)
    zero_hist = jnp.zeros((DEPTH, n_prompt, POOL_HIST, D_POOL), x_prompt.dtype)
    zero_state = jnp.zeros((DEPTH, n_prompt, N_RET_HEADS, DK_RET, DK_RET), F32)
    y_p, pool_p, ret_p = _trunk(x_prompt, mods[:, :n_prompt], zero_hist, zero_state, 0, wts,
                                PROMPT_TILES)
    y_s, pool_s, ret_s = _trunk(x_sample, mods[:, n_prompt:n_seq], cache_pool, state_ret,
                                PAST_LEN, wts, SAMPLE_TILES)
    return (y_p, y_s, pool_p, ret_p.astype(state_ret.dtype), pool_s, ret_s.astype(state_ret.dtype))
```

```python
import functools
import math

import jax
import jax.numpy as jnp
from jax import lax
from jax.experimental import pallas as pl
from jax.experimental.pallas import tpu as pltpu

F32 = jnp.float32
BF16 = jnp.bfloat16

D_MODEL = 1024
DEPTH = 2
PAST_LEN = 2048
POOL_WINDOWS = (2, 4, 8, 16)
D_POOL = 512
POOL_GROUP = 128
POOL_HIST = 15
HIST_ROWS = 16
N_RET_HEADS = 8
DK_RET = 128
D_RET = N_RET_HEADS * DK_RET
RET_SCALE = DK_RET ** -0.5
ROPE_BASE = 10000.0
PEER_HEADS = 8
PEER_NKEYS = 128
PEER_N = PEER_NKEYS * PEER_NKEYS
PEER_DQ = 256
PEER_DHALF = 128
PEER_TOPK = 16
LN_EPS = 1e-5
ALPHA = (2 * DEPTH) ** 0.25
OFF_Q = D_POOL
OFF_K = OFF_Q + D_RET
OFF_V = OFF_K + D_RET
OFF_G = OFF_V + D_RET
OFF_BG = OFF_G + D_RET
D_IN = OFF_BG + 2 * D_MODEL
LOG_G = tuple(math.log(1.0 - 2.0 ** (-5.0 - h)) for h in range(N_RET_HEADS))
LANES = 128
SUBLANES = 8
KEY_TILES = PEER_NKEYS // SUBLANES
PACKED_ROWS = 16
ACT_PIECES = 2
ACT_LAG = 1
VMEM_LIMIT = 60 * 1024 * 1024
ADA_COLS = 1536
PEER_CHUNK_KEYS = 16
PROMPT_TILES = (256, 512)
SAMPLE_TILES = (32, 256)
NEG_INF = float("-inf")


def _ln(x):
    mu = jnp.mean(x, axis=-1, keepdims=True)
    xc = x - mu
    var = jnp.mean(xc * xc, axis=-1, keepdims=True)
    return xc * lax.rsqrt(var + LN_EPS)


def _gelu_tanh(x):
    a0 = -2.0 * math.sqrt(2.0 / math.pi) * math.log2(math.e)
    return x / (1.0 + jnp.exp2(x * (a0 + (a0 * 0.044715) * (x * x))))


def _params(*sem):
    return pltpu.CompilerParams(dimension_semantics=sem, vmem_limit_bytes=VMEM_LIMIT)


def _ada_kernel(c_ref, w_ref, b_ref, o_ref):
    c = c_ref[...]
    a = (c * jax.nn.sigmoid(c)).astype(BF16)
    o_ref[...] = jnp.dot(a, w_ref[...].astype(BF16), preferred_element_type=F32) + b_ref[...]


def _ada(c_pad, w_ada, b_ada):
    rows = c_pad.shape[0]
    n_out = w_ada.shape[-1]
    ct = ADA_COLS
    return pl.pallas_call(
        _ada_kernel,
        grid=(DEPTH, n_out // ct),
        in_specs=[
            pl.BlockSpec((rows, D_MODEL), lambda l, j: (0, 0)),
            pl.BlockSpec((None, D_MODEL, ct), lambda l, j: (l, 0, j)),
            pl.BlockSpec((None, 1, ct), lambda l, j: (l, 0, j)),
        ],
        out_specs=pl.BlockSpec((None, rows, ct), lambda l, j: (l, 0, j)),
        out_shape=jax.ShapeDtypeStruct((DEPTH, rows, n_out), F32),
        compiler_params=_params("arbitrary", "arbitrary"),
        name="ada",
    )(c_pad, w_ada, b_ada.reshape(DEPTH, 1, n_out))


def _mod_spec(per_token, tt, tiles_per_seq):
    if per_token:
        return pl.BlockSpec((1, tt, D_MODEL), lambda i, *_: (0, i, 0))
    return pl.BlockSpec((1, 1, D_MODEL), lambda i, *_: (i // tiles_per_seq, 0, 0))


MXU_COLS = 256


class _LazyProjection:
    def __init__(self, h_ref, w_ref):
        self.h_ref, self.w_ref, self.blocks = h_ref, w_ref, {}

    def _cols(self, lo, hi):
        return jnp.dot(self.h_ref[...], self.w_ref[:, lo:hi], preferred_element_type=F32)

    def __getitem__(self, idx):
        lo, hi = idx[1].start, idx[1].stop
        if hi - lo >= MXU_COLS:
            return self._cols(lo, hi)
        base = lo // MXU_COLS * MXU_COLS
        if base not in self.blocks:
            self.blocks[base] = self._cols(base, base + MXU_COLS)
        return self.blocks[base][:, lo - base:hi - base]

def _mix_kernel(x_ref, sc_ref, sh_ref, g1_ref, cos_ref, sin_ref, hist_ref, s0_ref, win_ref,
                wgrp_ref, pscale_ref, wbp_ref, wbr_ref, wout_ref, lng_ref, lnb_ref,
                x1_ref, sfin_ref, hout_ref,
                h_scr, state_scr, ext_scr, decay_scr, xi_scr, zeta_scr, retg_scr, pool_scr,
                *, tt, pos0, n_tiles):
    j = pl.program_id(1)

    h_scr[...] = (_ln(x_ref[0]) * (1.0 + sc_ref[0]) + sh_ref[0]).astype(BF16)
    proj_ref = _LazyProjection(h_scr, win_ref)

    @pl.when(j == 0)
    def _init():
        state_scr[...] = s0_ref[0]
        ext_scr[0:HIST_ROWS, :] = hist_ref[0]
        row = lax.broadcasted_iota(jnp.int32, (tt, tt), 0)
        col = lax.broadcasted_iota(jnp.int32, (tt, tt), 1)
        causal = row >= col
        diff = jnp.where(causal, (row - col).astype(F32), 0.0)
        rowl = lax.broadcasted_iota(jnp.int32, (tt, LANES), 0).astype(F32)
        for h in range(N_RET_HEADS):
            decay_scr[h] = jnp.where(causal, jnp.exp(LOG_G[h] * diff), 0.0)
            xi_scr[h] = jnp.exp(LOG_G[h] * (rowl + 1.0))
            zeta_scr[h] = jnp.exp(LOG_G[h] * (tt - 1.0 - rowl))

    p = proj_ref[:,0:D_POOL]
    ext_scr[HIST_ROWS:HIST_ROWS + tt, :] = p
    pos = pos0 + j * tt + lax.broadcasted_iota(jnp.int32, (tt, POOL_GROUP), 0)
    for g, w in enumerate(POOL_WINDOWS):
        cs = slice(g * POOL_GROUP, (g + 1) * POOL_GROUP)
        ws = ext_scr[HIST_ROWS:HIST_ROWS + tt, cs]
        for d in range(1, w):
            ws = ws + ext_scr[HIST_ROWS - d:HIST_ROWS - d + tt, cs]
        cnt = jnp.minimum(pos + 1, w).astype(F32)
        pooled = ws / cnt - p[:, cs]
        mixed = jnp.dot(pooled.astype(BF16), wgrp_ref[g], preferred_element_type=F32)
        pool_scr[:, cs] = (mixed * pscale_ref[:, cs]).astype(BF16)
    ext_scr[0:HIST_ROWS, :] = ext_scr[tt:tt + HIST_ROWS, :]

    cosv = cos_ref[...]
    sinv = sin_ref[...]
    for h in range(N_RET_HEADS):
        hs = slice(h * DK_RET, (h + 1) * DK_RET)
        q = proj_ref[:,OFF_Q + h * DK_RET:OFF_Q + (h + 1) * DK_RET]
        k = proj_ref[:,OFF_K + h * DK_RET:OFF_K + (h + 1) * DK_RET]
        v = proj_ref[:,OFF_V + h * DK_RET:OFF_V + (h + 1) * DK_RET].astype(BF16)
        g = proj_ref[:,OFF_G + h * DK_RET:OFF_G + (h + 1) * DK_RET]
        qr = q * cosv + pltpu.roll(q, DK_RET // 2, 1) * sinv
        kr = (k * cosv + pltpu.roll(k, DK_RET // 2, 1) * sinv) * RET_SCALE
        qb = qr.astype(BF16)
        scores = lax.dot_general(qb, kr.astype(BF16), (((1,), (1,)), ((), ())),
                                 preferred_element_type=F32) * decay_scr[h]
        intra = jnp.dot(scores.astype(BF16), v, preferred_element_type=F32)
        s_prev = state_scr[h]
        cross = jnp.dot(qb, s_prev.astype(BF16), preferred_element_type=F32) * xi_scr[h]
        kz = (kr * zeta_scr[h]).astype(BF16)
        kv = lax.dot_general(kz, v, (((0,), (0,)), ((), ())), preferred_element_type=F32)
        state_scr[h] = math.exp(LOG_G[h] * tt) * s_prev + kv
        retg_scr[:, hs] = (_ln(intra + cross) * (g * jax.nn.sigmoid(g))).astype(BF16)

    bgp = jax.nn.sigmoid(proj_ref[:,OFF_BG:OFF_BG + D_MODEL])
    bgr = jax.nn.sigmoid(proj_ref[:,OFF_BG + D_MODEL:OFF_BG + 2 * D_MODEL])
    pb = jnp.dot(pool_scr[...], wbp_ref[...], preferred_element_type=F32)
    rb = jnp.dot(retg_scr[...], wbr_ref[...], preferred_element_type=F32)
    merged = (bgp * pb + bgr * rb).astype(BF16)
    z = jnp.dot(merged, wout_ref[...], preferred_element_type=F32)
    x1_ref[0] = _ln(ALPHA * x_ref[0] + g1_ref[0] * z) * lng_ref[...] + lnb_ref[...]

    @pl.when(j == n_tiles - 1)
    def _fin():
        sfin_ref[0] = state_scr[...]
        hout_ref[0] = ext_scr[0:HIST_ROWS, :]


def _mix(x, sc1, sh1, g1, cos2, sin2, hist16, s0, wts, layer, tt, pos0):
    bsz, seq, _ = x.shape
    n_tiles = seq // tt
    w_in_b, wgrp_b, pscale, wbp_b, wbr_b, wout_b, ln1_g, ln1_b = wts
    const2 = lambda b, j: (layer, 0, 0)
    once = pl.Buffered(1)
    seq_row = pl.BlockSpec((1, 1, D_MODEL), lambda b, j: (b, 0, 0))
    return pl.pallas_call(
        functools.partial(_mix_kernel, tt=tt, pos0=pos0, n_tiles=n_tiles),
        grid=(bsz, n_tiles),
        in_specs=[
            pl.BlockSpec((1, tt, D_MODEL), lambda b, j: (b, j, 0)),
            seq_row, seq_row, seq_row,
            pl.BlockSpec((tt, LANES), lambda b, j: (j, 0)),
            pl.BlockSpec((tt, LANES), lambda b, j: (j, 0)),
            pl.BlockSpec((1, HIST_ROWS, D_POOL), lambda b, j: (b, 0, 0)),
            pl.BlockSpec((1, N_RET_HEADS, DK_RET, DK_RET), lambda b, j: (b, 0, 0, 0)),
            pl.BlockSpec((None, D_MODEL, D_IN), const2, pipeline_mode=once),
            pl.BlockSpec((None, len(POOL_WINDOWS), POOL_GROUP, POOL_GROUP),
                         lambda b, j: (layer, 0, 0, 0)),
            pl.BlockSpec((None, 1, D_POOL), const2),
            pl.BlockSpec((None, D_POOL, D_MODEL), const2, pipeline_mode=once),
            pl.BlockSpec((None, D_RET, D_MODEL), const2, pipeline_mode=once),
            pl.BlockSpec((None, D_MODEL, D_MODEL), const2, pipeline_mode=once),
            pl.BlockSpec((None, 1, D_MODEL), const2),
            pl.BlockSpec((None, 1, D_MODEL), const2),
        ],
        out_specs=[
            pl.BlockSpec((1, tt, D_MODEL), lambda b, j: (b, j, 0)),
            pl.BlockSpec((1, N_RET_HEADS, DK_RET, DK_RET), lambda b, j: (b, 0, 0, 0)),
            pl.BlockSpec((1, HIST_ROWS, D_POOL), lambda b, j: (b, 0, 0)),
        ],
        out_shape=[
            jax.ShapeDtypeStruct((bsz, seq, D_MODEL), F32),
            jax.ShapeDtypeStruct((bsz, N_RET_HEADS, DK_RET, DK_RET), F32),
            jax.ShapeDtypeStruct((bsz, HIST_ROWS, D_POOL), F32),
        ],
        scratch_shapes=[
            pltpu.VMEM((tt, D_MODEL), BF16),
            pltpu.VMEM((N_RET_HEADS, DK_RET, DK_RET), F32),
            pltpu.VMEM((HIST_ROWS + tt, D_POOL), F32),
            pltpu.VMEM((N_RET_HEADS, tt, tt), F32),
            pltpu.VMEM((N_RET_HEADS, tt, LANES), F32),
            pltpu.VMEM((N_RET_HEADS, tt, LANES), F32),
            pltpu.VMEM((tt, D_RET), BF16),
            pltpu.VMEM((tt, D_POOL), BF16),
        ],
        compiler_params=_params("arbitrary", "arbitrary"),
        name="mix",
    )(x, sc1, sh1, g1, cos2, sin2, hist16, s0, w_in_b, wgrp_b, pscale, wbp_b, wbr_b, wout_b,
      ln1_g, ln1_b)


def _sort_tiles_desc(tiles):
    a = list(tiles)
    n = len(a)
    k = 2
    while k <= n:
        j = k // 2
        while j >= 1:
            for i in range(n):
                l = i ^ j
                if l > i:
                    hi = jnp.maximum(a[i], a[l])
                    lo = jnp.minimum(a[i], a[l])
                    a[i], a[l] = (hi, lo) if (i & k) == 0 else (lo, hi)
            j //= 2
        k *= 2
    return a


def _merge_top16(lists, extra=None):
    cur = list(lists)
    rows = []
    for r in range(PEER_TOPK):
        m = jnp.max(cur[0], axis=0, keepdims=True)
        if extra is not None:
            m = jnp.maximum(m, jnp.max(extra, axis=0, keepdims=True))
        rows.append(m)
        if r + 1 < PEER_TOPK:
            hit = cur[0] == m
            cur = [jnp.where(hit, cur[j + 1], cur[j]) for j in range(PEER_TOPK - 1 - r)]
            if extra is not None:
                extra = jnp.where(extra == m, NEG_INF, extra)
    return rows


def _top16_rows(s):
    tiles = [s[k * SUBLANES:(k + 1) * SUBLANES, :] for k in range(KEY_TILES)]
    return _merge_top16(_sort_tiles_desc(tiles))


def _stack_rows(rows, tt):
    ridx = lax.broadcasted_iota(jnp.int32, (len(rows), tt), 0)
    out = jnp.zeros((len(rows), tt), F32)
    for i, r in enumerate(rows):
        out = jnp.where(ridx == i, r, out)
    return out


def _activations(u_rows, h2_scr, s_dst, rows, n_lg):
    s_val = jnp.dot(u_rows, h2_scr[...], preferred_element_type=F32)
    for lg in range(n_lg):
        s_dst[lg, rows, :] = s_val[:, lg * LANES:(lg + 1) * LANES]


def _peer_kernel(x1_ref, sc_ref, sh_ref, g2_ref, wqt_ref, keys_ref, u_ref, vt_ref,
                 lng_ref, lnb_ref, o_ref,
                 h2_scr, c_scr, e1_scr, s2_scr, e2_scr, cb_scr, eb_scr, s_a, s_b, ga_scr, acc_scr,
                 *, tt, na, n_chunks):
    e = pl.program_id(1)
    n_lg = tt // LANES

    @pl.when(e == 0)
    def _prep():
        h2 = _ln(x1_ref[...]) * (1.0 + sc_ref[0]) + sh_ref[0]
        h2_scr[...] = h2.T.astype(BF16)
        qt = jnp.dot(wqt_ref[...], h2_scr[...], preferred_element_type=F32)
        qt_scr = ga_scr
        qt_scr[...] = qt.astype(BF16)

        for h in range(PEER_HEADS):
            base = h * PEER_DQ
            c_scr[h] = jnp.dot(keys_ref[0], qt_scr[base:base + PEER_DHALF, :],
                               preferred_element_type=F32)
            e1_scr[h] = jnp.dot(keys_ref[1], qt_scr[base + PEER_DHALF:base + PEER_DQ, :],
                                preferred_element_type=F32)

        def head(h, carry):
            s1 = c_scr[h]
            s2 = e1_scr[h]
            r1 = _top16_rows(s1)
            r2 = _top16_rows(s2)
            v2_lo = _stack_rows(r2[:SUBLANES], tt)
            v2_hi = _stack_rows(r2[SUBLANES:], tt)
            cand = [r1[i] + v2_lo for i in range(PEER_TOPK)]
            singles = r1[0] + v2_hi
            tau = _merge_top16(cand, singles)[PEER_TOPK - 1]
            top = r1[0] + r2[0]
            zsum = jnp.zeros((SUBLANES, tt), F32)
            for c in cand + [singles]:
                zsum = zsum + jnp.where(c >= tau, jnp.exp(c - top), 0.0)
            z = jnp.sum(zsum, axis=0, keepdims=True)
            c_scr[h] = tau - s1
            e1_scr[h] = jnp.exp(s1 - r1[0])
            e2 = jnp.exp(s2 - r2[0]) / z
            for lg in range(n_lg):
                ls = slice(lg * LANES, (lg + 1) * LANES)
                s2_scr[h, lg] = s2[:, ls].reshape(KEY_TILES, SUBLANES, LANES)
                e2_scr[h, lg] = e2[:, ls].reshape(KEY_TILES, SUBLANES, LANES)
            return carry

        lax.fori_loop(0, PEER_HEADS, head, 0)

        acc_scr[...] = jnp.zeros_like(acc_scr)
        _activations(u_ref[...], h2_scr, s_a, slice(None), n_lg)

    def step(s_w, s_r):
        a0 = pl.multiple_of((e - 1) * na, na)
        for h in range(PEER_HEADS):
            cslab = c_scr[h, pl.ds(a0, na), :]
            eslab = e1_scr[h, pl.ds(a0, na), :]
            for r in range(na):
                cb = jnp.broadcast_to(cslab[r:r + 1, :], (SUBLANES, tt))
                eb = jnp.broadcast_to(eslab[r:r + 1, :], (SUBLANES, tt))
                for lg in range(n_lg):
                    cb_scr[h, r, lg] = cb[:, lg * LANES:(lg + 1) * LANES]
                    eb_scr[h, r, lg] = eb[:, lg * LANES:(lg + 1) * LANES]

        halves = [tuple(range(lo, min(lo + 2, n_lg))) for lo in range(0, n_lg, 2)]
        groups = [(half, ai, lg) for half in halves for lg in half for ai in range(na)]
        act_at = [ACT_LAG + p * (len(groups) // ACT_PIECES) for p in range(ACT_PIECES)]
        rows_per = na * PEER_NKEYS // ACT_PIECES
        ga = None
        for gi, (half, ai, lg) in enumerate(groups):
            if s_w is not None and gi in act_at:
                p = act_at.index(gi)
                start = p * rows_per
                if ga is not None:
                    probe = jnp.max(ga[0:SUBLANES, :])
                    other = (start + rows_per) % (na * PEER_NKEYS)
                    start = jnp.where(probe != probe, other, start)
                start = pl.multiple_of(start, PACKED_ROWS)
                _activations(u_ref[pl.ds(start, rows_per), :], h2_scr, s_w,
                             pl.ds(start, rows_per), n_lg)
            gate = jnp.zeros((KEY_TILES, SUBLANES, LANES), F32)
            for h in range(PEER_HEADS):
                hit = s2_scr[h, lg] >= cb_scr[h, ai, lg][None]
                gate = gate + jnp.where(hit, e2_scr[h, lg], 0.0) * eb_scr[h, ai, lg][None]
            act = _gelu_tanh(s_r[lg, ai * PEER_NKEYS:(ai + 1) * PEER_NKEYS, :])
            ga = gate.reshape(PEER_NKEYS, LANES) * act
            ga_scr[ai * PEER_NKEYS:(ai + 1) * PEER_NKEYS, lg * LANES:(lg + 1) * LANES] = ga.astype(BF16)
            if ai == na - 1 and lg == half[-1]:
                hs = slice(half[0] * LANES, (half[-1] + 1) * LANES)
                acc_scr[:, hs] += jnp.dot(vt_ref[...], ga_scr[:, hs], preferred_element_type=F32)

    assert n_chunks % 2 == 0

    @pl.when(e % 2 == 1)
    def _odd():
        step(s_b, s_a)

    @pl.when((e > 0) & (e % 2 == 0) & (e < n_chunks))
    def _even():
        step(s_a, s_b)

    @pl.when(e == n_chunks)
    def _last():
        step(None, s_b)
        y = acc_scr[...].T
        o_ref[...] = _ln(ALPHA * x1_ref[...] + g2_ref[0] * y) * lng_ref[...] + lnb_ref[...]


def _peer(x1_flat, sc, sh, g2, wts, layer, tt, per_token, tiles_per_seq):
    t = x1_flat.shape[0]
    wqt_b, keys_b, u_b, vt_b, ln2_g, ln2_b = wts
    na = PEER_CHUNK_KEYS
    ec = na * PEER_NKEYS
    assert ec == PEER_HEADS * PEER_DQ
    n_chunks = PEER_N // ec
    n_lg = tt // LANES
    mod = _mod_spec(per_token, tt, tiles_per_seq)
    return pl.pallas_call(
        functools.partial(_peer_kernel, tt=tt, na=na, n_chunks=n_chunks),
        grid=(t // tt, n_chunks + 1),
        in_specs=[
            pl.BlockSpec((tt, D_MODEL), lambda i, e: (i, 0)),
            mod, mod, mod,
            pl.BlockSpec((None, PEER_HEADS * PEER_DQ, D_MODEL), lambda i, e: (layer, 0, 0)),
            pl.BlockSpec((None, 2, PEER_NKEYS, PEER_DHALF), lambda i, e: (layer, 0, 0, 0)),
            pl.BlockSpec((None, ec, D_MODEL),
                         lambda i, e: (layer, jnp.minimum(e, n_chunks - 1), 0)),
            pl.BlockSpec((None, D_MODEL, ec), lambda i, e: (layer, 0, jnp.maximum(e - 1, 0))),
            pl.BlockSpec((None, 1, D_MODEL), lambda i, e: (layer, 0, 0)),
            pl.BlockSpec((None, 1, D_MODEL), lambda i, e: (layer, 0, 0)),
        ],
        out_specs=pl.BlockSpec((tt, D_MODEL), lambda i, e: (i, 0)),
        out_shape=jax.ShapeDtypeStruct((t, D_MODEL), F32),
        scratch_shapes=[
            pltpu.VMEM((D_MODEL, tt), BF16),
            pltpu.VMEM((PEER_HEADS, PEER_NKEYS, tt), F32),
            pltpu.VMEM((PEER_HEADS, PEER_NKEYS, tt), F32),
            pltpu.VMEM((PEER_HEADS, n_lg, KEY_TILES, SUBLANES, LANES), F32),
            pltpu.VMEM((PEER_HEADS, n_lg, KEY_TILES, SUBLANES, LANES), F32),
            pltpu.VMEM((PEER_HEADS, na, n_lg, SUBLANES, LANES), F32),
            pltpu.VMEM((PEER_HEADS, na, n_lg, SUBLANES, LANES), F32),
            pltpu.VMEM((n_lg, ec, LANES), F32),
            pltpu.VMEM((n_lg, ec, LANES), F32),
            pltpu.VMEM((ec, tt), BF16),
            pltpu.VMEM((D_MODEL, tt), F32),
        ],
        compiler_params=_params("arbitrary", "arbitrary"),
        name="peer",
    )(x1_flat, sc, sh, g2, wqt_b, keys_b, u_b, vt_b, ln2_g, ln2_b)


def _rope_tables(pos0, seq):
    half = DK_RET // 2
    inv = ROPE_BASE ** (-jnp.arange(half, dtype=F32) / half)
    ang = (pos0 + jnp.arange(seq)).astype(F32)[:, None] * inv[None, :]
    cos = jnp.cos(ang)
    sin = jnp.sin(ang)
    return jnp.concatenate([cos, cos], axis=-1), jnp.concatenate([-sin, sin], axis=-1)


def _trunk(x, mods, pool_hist, ret_state, pos0, wts, tiles):
    bsz, seq, _ = x.shape
    tt_mix, tt_peer = tiles
    per_token = seq < tt_peer
    cos2, sin2 = _rope_tables(pos0, seq)
    hist16 = jnp.pad(pool_hist, ((0, 0), (0, 0), (HIST_ROWS - POOL_HIST, 0), (0, 0)))
    new_hist, new_state = [], []
    for l in range(DEPTH):
        chunks = [mods[l, :, k * D_MODEL:(k + 1) * D_MODEL] for k in range(6)]
        seq_rows = [c[:, None, :] for c in chunks]
        if per_token:
            tok_rows = [jnp.repeat(c, seq, axis=0)[None] for c in chunks]
        else:
            tok_rows = seq_rows
        _, _, _, sh2, sc2, g2 = tok_rows
        sh1, sc1, g1 = seq_rows[:3]
        x1, s_l, hist_l = _mix(x, sc1, sh1, g1, cos2, sin2, hist16[l], ret_state[l],
                               (wts["w_in"], wts["w_grp"], wts["pool_scale"], wts["w_bp"],
                                wts["w_br"], wts["w_out"], wts["ln1_g"], wts["ln1_b"]),
                               l, tt_mix, pos0)
        x2 = _peer(x1.reshape(bsz * seq, D_MODEL), sc2, sh2, g2,
                   (wts["w_qt"], wts["keys"], wts["u"], wts["vt"], wts["ln2_g"], wts["ln2_b"]),
                   l, tt_peer, per_token, seq // tt_peer if not per_token else 1)
        x = x2.reshape(bsz, seq, D_MODEL)
        new_hist.append(hist_l[:, HIST_ROWS - POOL_HIST:, :])
        new_state.append(s_l)
    return x, jnp.stack(new_hist), jnp.stack(new_state)


def kernel(x_prompt, x_sample, cache_pool, state_ret, c_prompt, c_sample, w_ada, b_ada, w_in,
           w_pool_grp, pool_scale, w_branch_pool, w_branch_ret, w_out, ln1_g, ln1_b, w_peer_q,
           peer_sub_keys, peer_u, peer_v, ln2_g, ln2_b):
    n_prompt = x_prompt.shape[0]
    n_sample = x_sample.shape[0]
    wts = {
        "w_in": w_in.astype(BF16),
        "w_grp": w_pool_grp.astype(BF16),
        "pool_scale": pool_scale[:, None, :],
        "w_bp": w_branch_pool.astype(BF16),
        "w_br": w_branch_ret.astype(BF16),
        "w_out": w_out.astype(BF16),
        "ln1_g": ln1_g[:, None, :],
        "ln1_b": ln1_b[:, None, :],
        "w_qt": jnp.swapaxes(w_peer_q, 1, 2).astype(BF16),
        "keys": peer_sub_keys.astype(BF16),
        "u": peer_u.astype(BF16),
        "vt": jnp.swapaxes(peer_v, 1, 2).astype(BF16),
        "ln2_g": ln2_g[:, None, :],
        "ln2_b": ln2_b[:, None, :],
    }
    n_seq = n_prompt + n_sample
    c_pad = jnp.pad(jnp.concatenate([c_prompt, c_sample], axis=0), ((0, 16 - n_seq), (0, 0)))
    mods = _ada(c_pad, w_ada, b_ada)
    zero_hist = jnp.zeros((DEPTH, n_prompt, POOL_HIST, D_POOL), x_prompt.dtype)
    zero_state = jnp.zeros((DEPTH, n_prompt, N_RET_HEADS, DK_RET, DK_RET), F32)
    y_p, pool_p, ret_p = _trunk(x_prompt, mods[:, :n_prompt], zero_hist, zero_state, 0, wts,
                                PROMPT_TILES)
    y_s, pool_s, ret_s = _trunk(x_sample, mods[:, n_prompt:n_seq], cache_pool, state_ret,
                                PAST_LEN, wts, SAMPLE_TILES)
    return (y_p, y_s, pool_p, ret_p.astype(state_ret.dtype), pool_s, ret_s.astype(state_ret.dtype))
```
